```python
import math
import jax, jax.numpy as jnp
from jax import lax
import numpy as np

D_MODEL = 2048
BATCH = 16
SEQ = 2048
DEPTH = 2

CTX_LEN = 256
GRID_W = 64
BLK = 128
EPS = 1e-6
N_MOD = 6

D_A = 2048
H_A = 8
DH_A = D_A // H_A

D_INNER = 2048
HEAD_P = 64
H_B = D_INNER // HEAD_P
G_B = 4
R_B = H_B // G_B
N_STATE = 128
CONV_W = 5
CONV_DIM = D_INNER + 2 * G_B * N_STATE

HEAD_DIM = 128
H_C = 8
KV_C = 2
H_D = 8
KV_D = 2
WINDOW = 128
ROPE_BASE = 10000.0
AXIS_DIM = HEAD_DIM // 2

D_FF = 4 * D_MODEL

AB_SPLITS = (2 * D_A, 2 * D_A + D_INNER, 2 * D_A + D_INNER + CONV_DIM)
AB_IN = 2 * D_A + D_INNER + CONV_DIM + 2 * H_B
AB_OUT_IN = D_A + D_INNER

QC_END = H_C * HEAD_DIM
KC_END = QC_END + KV_C * HEAD_DIM
VC_END = KC_END + KV_C * HEAD_DIM
QD_END = VC_END + H_D * HEAD_DIM
KD_END = QD_END + KV_D * HEAD_DIM
CD_IN = KD_END + KV_D * HEAD_DIM
CD_OUT_IN = (H_C + H_D) * HEAD_DIM

kernel_name = 'hybrid_dit_gmlp_ssd_axialgqa_swa'


def rmsnorm(x, g):
    x32 = x.astype(jnp.float32)
    y = x32 * lax.rsqrt(jnp.mean(x32 * x32, axis=-1, keepdims=True) + EPS)
    return (y * g.astype(jnp.float32)).astype(x.dtype)


def layernorm(x, g, b):
    x32 = x.astype(jnp.float32)
    mu = jnp.mean(x32, axis=-1, keepdims=True)
    xc = x32 - mu
    y = xc * lax.rsqrt(jnp.mean(xc * xc, axis=-1, keepdims=True) + EPS)
    return (y * g.astype(jnp.float32) + b.astype(jnp.float32)).astype(x.dtype)


def modulate(h, g, shift, scale):
    return rmsnorm(h, g) * (1 + scale) + shift


def mlp_sublayer(h, g_pre, g_post, shift, scale, gate, w1, w2):
    a = modulate(h, g_pre, shift, scale)
    y = jnp.square(jax.nn.relu(a @ w1)) @ w2
    return h + gate * rmsnorm(y, g_post)


def chunk_gmlp(p, w_s, b_s, ln_g, ln_b):
    bsz, L, _ = p.shape
    z = jax.nn.gelu(p, approximate=False)
    u, v = jnp.split(z, 2, axis=-1)
    v = layernorm(v, ln_g, ln_b).reshape(bsz, L // BLK, BLK, H_A, DH_A)
    v = jnp.einsum('hpq,bnqhd->bnphd', w_s, v) + b_s.T[:, :, None]
    return u * v.reshape(bsz, L, D_A)


def depthwise_conv_centred(x, w, b):
    y = lax.conv_general_dilated(
        x, w[:, None, :].astype(x.dtype), window_strides=(1,),
        padding=[(CONV_W // 2, CONV_W // 2)],
        dimension_numbers=('NWC', 'WIO', 'NWC'), feature_group_count=x.shape[-1])
    return y + b


def ssd_scan(xs, dt, a, bs, cs, h0):
    f32 = jnp.float32
    bsz, L = xs.shape[:2]
    nc = L // BLK
    x = xs.astype(f32).reshape(bsz, nc, BLK, G_B, R_B, HEAD_P)
    dtc = dt.astype(f32).reshape(bsz, nc, BLK, G_B, R_B)
    b = bs.astype(f32).reshape(bsz, nc, BLK, G_B, N_STATE)
    c = cs.astype(f32).reshape(bsz, nc, BLK, G_B, N_STATE)
    xdt = x * dtc[..., None]
    cum = jnp.cumsum(jnp.moveaxis(dtc * a, 2, -1), axis=-1)
    seg = cum[..., :, None] - cum[..., None, :]
    lower = jnp.tril(jnp.ones((BLK, BLK), dtype=bool))
    decay = jnp.where(lower, jnp.exp(jnp.where(lower, seg, 0.0)), 0.0)
    cb = jnp.einsum('bcign,bcjgn->bcgij', c, b)
    y_diag = jnp.einsum('bcgij,bcgrij,bcjgrp->bcigrp', cb, decay, xdt)
    to_end = jnp.exp(cum[..., -1:] - cum)
    states = jnp.einsum('bcjgn,bcgrj,bcjgrp->bcgrpn', b, to_end, xdt)
    chunk_decay = jnp.exp(cum[..., -1])

    def step(h, inp):
        s, d = inp
        return h * d[..., None, None] + s, h

    h_last, h_in = lax.scan(step, h0.astype(f32),
                            (jnp.moveaxis(states, 1, 0), jnp.moveaxis(chunk_decay, 1, 0)))
    h_in = jnp.moveaxis(h_in, 0, 1)
    y_off = jnp.einsum('bcign,bcgrpn,bcgri->bcigrp', c, h_in, jnp.exp(cum))
    y = (y_diag + y_off).reshape(xs.shape)
    return y.astype(xs.dtype), h_last


def ssd_branch(z, xbc, dt_raw, conv_w, conv_b, a_log, dt_bias, d_skip, gn_g, h0):
    bsz, L, _ = z.shape
    xbc = jax.nn.silu(depthwise_conv_centred(xbc, conv_w, conv_b))
    xs, bs, cs = jnp.split(xbc, (D_INNER, D_INNER + G_B * N_STATE), axis=-1)
    xs = xs.reshape(bsz, L, G_B, R_B, HEAD_P)
    bs = bs.reshape(bsz, L, G_B, N_STATE)
    cs = cs.reshape(bsz, L, G_B, N_STATE)
    dt = jax.nn.softplus(dt_raw.astype(jnp.float32).reshape(bsz, L, 2, G_B, R_B)
                         + dt_bias.astype(jnp.float32).reshape(2, G_B, R_B))
    a = -jnp.exp(a_log.astype(jnp.float32)).reshape(2, G_B, R_B)
    if h0 is None:
        zero = jnp.zeros((bsz, G_B, R_B, HEAD_P, N_STATE), jnp.float32)
        h0 = (zero, zero)
    flip = lambda t: jnp.flip(t, axis=1)
    y_f, h_f = ssd_scan(xs, dt[:, :, 0], a[0], bs, cs, h0[0])
    y_b, h_b = ssd_scan(flip(xs), flip(dt[:, :, 1]), a[1], flip(bs), flip(cs), h0[1])
    y = y_f + flip(y_b) + d_skip.reshape(G_B, R_B)[..., None] * xs
    y = y.reshape(bsz, L, D_INNER) * jax.nn.silu(z)
    y = rmsnorm(y.reshape(bsz, L, G_B, D_INNER // G_B), gn_g.reshape(G_B, -1))
    return y.reshape(bsz, L, D_INNER), (h_f, h_b)


def ab_mixer(hx, hc, w_in, w_s, b_s, ln_g, ln_b, conv_w, conv_b, a_log, dt_bias, d_skip, gn_g,
             w_out, need_ctx):
    pa_x, z_x, xbc_x, dt_x = jnp.split(hx @ w_in, AB_SPLITS, axis=-1)
    pa_c, z_c, xbc_c, dt_c = jnp.split(hc @ w_in, AB_SPLITS, axis=-1)
    yb_c, st_c = ssd_branch(z_c, xbc_c, dt_c, conv_w, conv_b, a_log, dt_bias, d_skip, gn_g, None)
    yb_x, _ = ssd_branch(z_x, xbc_x, dt_x, conv_w, conv_b, a_log, dt_bias, d_skip, gn_g, st_c)
    ya_x = chunk_gmlp(pa_x, w_s, b_s, ln_g, ln_b)
    y_x = jnp.concatenate([ya_x, yb_x], axis=-1) @ w_out
    y_c = None
    if need_ctx:
        ya_c = chunk_gmlp(pa_c, w_s, b_s, ln_g, ln_b)
        y_c = jnp.concatenate([ya_c, yb_c], axis=-1) @ w_out
    return y_x, y_c


def axial_rope(L):
    rows = L // GRID_W
    r_idx = jnp.repeat(jnp.arange(rows), GRID_W)
    c_idx = jnp.tile(jnp.arange(GRID_W), rows)
    pos = jnp.stack([r_idx, c_idx], axis=-1).astype(jnp.float32)
    inv_freq = ROPE_BASE ** (-jnp.arange(0, AXIS_DIM, 2, dtype=jnp.float32) / AXIS_DIM)
    ang = pos[:, :, None] * inv_freq
    return jnp.cos(ang), jnp.sin(ang)


def apply_rope(x, cos, sin):
    shp = x.shape
    xr = x.astype(jnp.float32).reshape(shp[:-1] + (2, 2, AXIS_DIM // 2))
    x1, x2 = xr[..., 0, :], xr[..., 1, :]
    bshape = (shp[1],) + (1,) * (x.ndim - 3) + (2, AXIS_DIM // 2)
    c = cos.reshape(bshape)
    s = sin.reshape(bshape)
    out = jnp.stack([x1 * c - x2 * s, x2 * c + x1 * s], axis=-2)
    return out.reshape(shp).astype(x.dtype)


def block_attn(q, k, v):
    bsz, L = q.shape[:2]
    scale = HEAD_DIM ** -0.5
    qb = jnp.moveaxis(q.reshape((bsz, L // BLK, BLK) + q.shape[2:]), 1, 0)

    def one(qblk):
        s = jnp.einsum('bqhrd,bkhd->bhrqk', qblk, k).astype(jnp.float32) * scale
        p = jax.nn.softmax(s, axis=-1).astype(v.dtype)
        return jnp.einsum('bhrqk,bkhd->bqhrd', p, v)

    o = lax.map(one, qb)
    return jnp.moveaxis(o, 0, 1).reshape(q.shape)


def window_sink_attn(q, k, v, k_ctx, v_ctx, sink):
    bsz, L = q.shape[:2]
    nb = L // BLK
    scale = HEAD_DIM ** -0.5

    def band(t):
        tb = jnp.pad(t, ((0, 0), (BLK, BLK), (0, 0), (0, 0))).reshape(bsz, nb + 2, BLK, t.shape[2], t.shape[3])
        return jnp.concatenate([tb[:, :-2], tb[:, 1:-1], tb[:, 2:]], axis=2)

    kb, vb = band(k), band(v)
    qb = q.reshape((bsz, nb, BLK) + q.shape[2:])
    s_loc = jnp.einsum('bnqhrd,bnkhd->bnhrqk', qb, kb).astype(jnp.float32) * scale
    blk = jnp.arange(nb)[:, None, None]
    qpos = blk * BLK + jnp.arange(BLK)[None, :, None]
    kpos = blk * BLK - BLK + jnp.arange(3 * BLK)[None, None, :]
    valid = (jnp.abs(kpos - qpos) <= WINDOW) & (kpos >= 0) & (kpos < L)
    s_loc = jnp.where(valid[None, :, None, None], s_loc, -jnp.inf)
    s_ctx = jnp.einsum('bnqhrd,bkhd->bnhrqk', qb, k_ctx).astype(jnp.float32) * scale
    s_sink = jnp.broadcast_to(sink.astype(jnp.float32)[None, None, :, :, None, None], s_loc.shape[:-1] + (1,))
    p = jax.nn.softmax(jnp.concatenate([s_loc, s_ctx, s_sink], axis=-1), axis=-1).astype(v.dtype)
    o = (jnp.einsum('bnhrqk,bnkhd->bnqhrd', p[..., :3 * BLK], vb)
         + jnp.einsum('bnhrqk,bkhd->bnqhrd', p[..., 3 * BLK:-1], v_ctx))
    return o.reshape(q.shape)


def sink_attn(q, k, v, sink):
    scale = HEAD_DIM ** -0.5
    s = jnp.einsum('bqhrd,bkhd->bhrqk', q, k).astype(jnp.float32) * scale
    s_sink = jnp.broadcast_to(sink.astype(jnp.float32)[None, :, :, None, None], s.shape[:-1] + (1,))
    p = jax.nn.softmax(jnp.concatenate([s, s_sink], axis=-1), axis=-1)[..., :-1].astype(v.dtype)
    return jnp.einsum('bhrqk,bkhd->bqhrd', p, v)


def cd_mixer(hx, hc, w_in, q_g, k_g, sink, w_out, need_ctx):
    bsz, L, _ = hx.shape
    Lc = hc.shape[1]
    qc, kc, vc, qd, kd, vd = jnp.split(hx @ w_in, (QC_END, KC_END, VC_END, QD_END, KD_END), axis=-1)
    cos, sin = axial_rope(L)
    qh = lambda t, n, kv: t.reshape(t.shape[0], t.shape[1], kv, n // kv, HEAD_DIM)
    kh = lambda t, kv: t.reshape(t.shape[0], t.shape[1], kv, HEAD_DIM)
    qc = apply_rope(rmsnorm(qh(qc, H_C, KV_C), q_g), cos, sin)
    kc = apply_rope(rmsnorm(kh(kc, KV_C), k_g), cos, sin)
    vc = kh(vc, KV_C)
    qd = apply_rope(qh(qd, H_D, KV_D), cos, sin)
    kd = apply_rope(kh(kd, KV_D), cos, sin)
    vd = kh(vd, KV_D)
    if need_ctx:
        qc_c, kc_c, vc_c, qd_c, kd_c, vd_c = jnp.split(hc @ w_in, (QC_END, KC_END, VC_END, QD_END, KD_END), axis=-1)
    else:
        kc_c, vc_c = jnp.split(hc @ w_in[:, QC_END:VC_END], 2, axis=-1)
        kd_c, vd_c = jnp.split(hc @ w_in[:, QD_END:], 2, axis=-1)
    kc_c = rmsnorm(kh(kc_c, KV_C), k_g)
    vc_c = kh(vc_c, KV_C)
    kd_c = kh(kd_c, KV_D)
    vd_c = kh(vd_c, KV_D)
    sink = sink.reshape(KV_D, H_D // KV_D)
    oc = block_attn(qc, jnp.concatenate([kc, kc_c], axis=1), jnp.concatenate([vc, vc_c], axis=1))
    od = window_sink_attn(qd, kd, vd, kd_c, vd_c, sink)
    y_x = jnp.concatenate([oc.reshape(bsz, L, -1), od.reshape(bsz, L, -1)], axis=-1) @ w_out
    y_c = None
    if need_ctx:
        qc_c = rmsnorm(qh(qc_c, H_C, KV_C), q_g)
        oc_c = block_attn(qc_c, kc_c, vc_c)
        od_c = sink_attn(qh(qd_c, H_D, KV_D), kd_c, vd_c, sink)
        y_c = jnp.concatenate([oc_c.reshape(bsz, Lc, -1), od_c.reshape(bsz, Lc, -1)], axis=-1) @ w_out
    return y_x, y_c


def setup_inputs(seed: int = 0) -> dict:
    key = jax.random.key(seed)
    ks = jax.random.split(key, 32)
    f32 = jnp.float32
    n_even = (DEPTH + 1) // 2
    n_odd = DEPTH // 2

    def nrm(k, shape, scale):
        return scale * jax.random.normal(k, shape, f32)

    dt0 = jnp.exp(jax.random.uniform(ks[15], (n_even, 2, H_B), f32, math.log(1e-3), math.log(1e-1)))
    return {
        'x': nrm(ks[0], (BATCH, SEQ, D_MODEL), 1.0),
        'c': nrm(ks[1], (BATCH, D_MODEL), 1.0),
        'ctx': nrm(ks[2], (BATCH, CTX_LEN, D_MODEL), 1.0),
        'c_ctx': nrm(ks[3], (D_MODEL,), 1.0),
        'mod_w': nrm(ks[4], (DEPTH, D_MODEL, N_MOD * D_MODEL), D_MODEL ** -0.5),
        'mod_b': nrm(ks[5], (DEPTH, N_MOD * D_MODEL), 0.02),
        'norm_g': 1.0 + nrm(ks[6], (DEPTH, 4, D_MODEL), 0.05),
        'mlp_w1': nrm(ks[7], (DEPTH, D_MODEL, D_FF), D_MODEL ** -0.5),
        'mlp_w2': nrm(ks[8], (DEPTH, D_FF, D_MODEL), D_FF ** -0.5),
        'ab_w_in': nrm(ks[9], (n_even, D_MODEL, AB_IN), D_MODEL ** -0.5),
        'a_w_s': nrm(ks[10], (n_even, H_A, BLK, BLK), BLK ** -0.5),
        'a_b_s': 1.0 + nrm(ks[11], (n_even, H_A, BLK), 0.1),
        'a_ln_g': 1.0 + nrm(ks[12], (n_even, D_A), 0.05),
        'a_ln_b': nrm(ks[13], (n_even, D_A), 0.02),
        'b_conv_w': nrm(ks[14], (n_even, CONV_W, CONV_DIM), CONV_W ** -0.5),
        'b_conv_b': nrm(ks[16], (n_even, CONV_DIM), 0.02),
        'b_a_log': jnp.log(jax.random.uniform(ks[17], (n_even, 2, H_B), f32, 1.0, 16.0)),
        'b_dt_bias': dt0 + jnp.log(-jnp.expm1(-dt0)),
        'b_d': 1.0 + nrm(ks[18], (n_even, H_B), 0.1),
        'b_norm_g': 1.0 + nrm(ks[19], (n_even, D_INNER), 0.05),
        'ab_w_out': nrm(ks[20], (n_even, AB_OUT_IN, D_MODEL), AB_OUT_IN ** -0.5),
        'cd_w_in': nrm(ks[21], (n_odd, D_MODEL, CD_IN), D_MODEL ** -0.5),
        'c_q_norm_g': 1.0 + nrm(ks[22], (n_odd, HEAD_DIM), 0.05),
        'c_k_norm_g': 1.0 + nrm(ks[23], (n_odd, HEAD_DIM), 0.05),
        'd_sink': nrm(ks[24], (n_odd, H_D), 0.5),
        'cd_w_out': nrm(ks[25], (n_odd, CD_OUT_IN, D_MODEL), CD_OUT_IN ** -0.5),
    }


def reference(x, c, ctx, c_ctx, mod_w, mod_b, norm_g, mlp_w1, mlp_w2, ab_w_in, a_w_s, a_b_s,
              a_ln_g, a_ln_b, b_conv_w, b_conv_b, b_a_log, b_dt_bias, b_d, b_norm_g, ab_w_out,
              cd_w_in, c_q_norm_g, c_k_norm_g, d_sink, cd_w_out):
    h_x, h_c = x, ctx
    s_lat = jax.nn.silu(c)
    s_ctx = jax.nn.silu(c_ctx)
    for i in range(DEPTH):
        last = i == DEPTH - 1
        j = i // 2
        sh1, sc1, g1, sh2, sc2, g2 = jnp.split((s_lat @ mod_w[i] + mod_b[i])[:, None, :], N_MOD, axis=-1)
        n_mc = 2 if last else N_MOD
        mc = jnp.split(s_ctx @ mod_w[i][:, :n_mc * D_MODEL] + mod_b[i][:n_mc * D_MODEL], n_mc, axis=-1)
        a_x = modulate(h_x, norm_g[i, 0], sh1, sc1)
        a_c = modulate(h_c, norm_g[i, 0], mc[0], mc[1])
        if i % 2 == 0:
            y_x, y_c = ab_mixer(a_x, a_c, ab_w_in[j], a_w_s[j], a_b_s[j], a_ln_g[j], a_ln_b[j],
                                b_conv_w[j], b_conv_b[j], b_a_log[j], b_dt_bias[j], b_d[j],
                                b_norm_g[j], ab_w_out[j], not last)
        else:
            y_x, y_c = cd_mixer(a_x, a_c, cd_w_in[j], c_q_norm_g[j], c_k_norm_g[j], d_sink[j],
                                cd_w_out[j], not last)
        h_x = h_x + g1 * rmsnorm(y_x, norm_g[i, 1])
        h_x = mlp_sublayer(h_x, norm_g[i, 2], norm_g[i, 3], sh2, sc2, g2, mlp_w1[i], mlp_w2[i])
        if not last:
            h_c = h_c + mc[2] * rmsnorm(y_c, norm_g[i, 1])
            h_c = mlp_sublayer(h_c, norm_g[i, 2], norm_g[i, 3], mc[3], mc[4], mc[5], mlp_w1[i], mlp_w2[i])
    return h_x
```

```python
import functools
import math

import jax
import jax.numpy as jnp
from jax import lax
from jax.experimental import pallas as pl
from jax.experimental.pallas import tpu as pltpu

F32 = jnp.float32
BF16 = jnp.bfloat16

D_MODEL = 2048
GRID_W = 64
BLK = 128
EPS = 1e-6
N_MOD = 6

D_A = 2048
H_A = 8
DH_A = D_A // H_A

D_INNER = 2048
HEAD_P = 64
H_B = D_INNER // HEAD_P
G_B = 4
R_B = H_B // G_B
N_STATE = 128
CONV_W = 5
CONV_DIM = D_INNER + 2 * G_B * N_STATE

HEAD_DIM = 128
H_C = 8
KV_C = 2
H_D = 8
KV_D = 2
ROPE_BASE = 10000.0
AXIS_DIM = HEAD_DIM // 2
D_FF = 4 * D_MODEL
CD_IN = (H_C + 2 * KV_C + H_D + 2 * KV_D) * HEAD_DIM
ATTN_SCALE = HEAD_DIM ** -0.5

AB_MAIN = 2 * D_A + D_INNER + CONV_DIM
DT_PAD = 128

VMEM_LIMIT_V7X = 56 * 1024 * 1024


def _params(*sem, vmem=VMEM_LIMIT_V7X):
    return pltpu.CompilerParams(dimension_semantics=sem, vmem_limit_bytes=vmem)


def _rms(x, g):
    return x * lax.rsqrt(jnp.mean(x * x, axis=-1, keepdims=True) + EPS) * g


def _silu(x):
    return x * jax.nn.sigmoid(x)


def _gelu(x):
    return 0.5 * x * (1.0 + lax.erf(x * (1.0 / math.sqrt(2.0))))


def _softplus(x):
    return jnp.maximum(x, 0.0) + jnp.log1p(jnp.exp(-jnp.abs(x)))


def _dot(a, b):
    return jnp.dot(a, b, preferred_element_type=F32)


def _dot_nt(a, b):
    return lax.dot_general(a, b, (((1,), (1,)), ((), ())), preferred_element_type=F32)


def _dot_tn(a, b):
    return lax.dot_general(a, b, (((0,), (0,)), ((), ())), preferred_element_type=F32)


def _modvec_body(s_ref, w_ref, b_ref, o_ref):
    a = _silu(s_ref[...]).astype(BF16)
    o_ref[...] = _dot(a, w_ref[...].astype(BF16)) + b_ref[...]


def _modvec(s, w, b):
    r, d = s.shape
    n = w.shape[1]
    tn = 1024
    return pl.pallas_call(
        _modvec_body,
        grid=(n // tn,),
        in_specs=[pl.BlockSpec((r, d), lambda j: (0, 0)),
                  pl.BlockSpec((d, tn), lambda j: (0, j)),
                  pl.BlockSpec((1, tn), lambda j: (0, j))],
        out_specs=pl.BlockSpec((r, tn), lambda j: (0, j)),
        out_shape=jax.ShapeDtypeStruct((r, n), F32),
        compiler_params=_params("arbitrary"),
        name="modvec",
    )(s, w, b.reshape(1, n))


def _nmm_body(x_ref, g_ref, mod_ref, w_ref, o_ref, a_ref, *, shift_idx, scale_idx):
    @pl.when(pl.program_id(1) == 0)
    def _():
        a = (_rms(x_ref[...], g_ref[...]) * (1.0 + mod_ref[0, scale_idx:scale_idx + 1, :])
             + mod_ref[0, shift_idx:shift_idx + 1, :])
        a_ref[...] = a.astype(BF16)

    o_ref[...] = _dot(a_ref[...], w_ref[...]).astype(o_ref.dtype)


def _nmm(x, g, mods, w, *, shift_idx, scale_idx, out_dtype, tm, tn):
    m, k = x.shape
    n = w.shape[1]
    tm = min(tm, m // mods.shape[0])
    tn = min(tn, n)
    rows_per_mod = m // mods.shape[0]
    assert rows_per_mod % tm == 0 and n % tn == 0 and m % tm == 0
    return pl.pallas_call(
        functools.partial(_nmm_body, shift_idx=shift_idx, scale_idx=scale_idx),
        grid=(m // tm, n // tn),
        in_specs=[pl.BlockSpec((tm, k), lambda i, j: (i, 0)),
                  pl.BlockSpec((1, k), lambda i, j: (0, 0)),
                  pl.BlockSpec((1, N_MOD, k), lambda i, j: ((i * tm) // rows_per_mod, 0, 0)),
                  pl.BlockSpec((k, tn), lambda i, j: (0, j))],
        out_specs=pl.BlockSpec((tm, tn), lambda i, j: (i, j)),
        out_shape=jax.ShapeDtypeStruct((m, n), out_dtype),
        scratch_shapes=[pltpu.VMEM((tm, k), BF16)],
        compiler_params=_params("parallel", "arbitrary"),
        name="norm_mod_matmul",
    )(x, g.reshape(1, k), mods, w)


def _gmlp_body(u_ref, v_ref, lng_ref, lnb_ref, ws_ref, bias_ref, o_ref, *, nchunk):
    v = _gelu(v_ref[...])
    mu = jnp.mean(v, axis=-1, keepdims=True)
    vc = v - mu
    vn = vc * lax.rsqrt(jnp.mean(vc * vc, axis=-1, keepdims=True) + EPS) * lng_ref[...] + lnb_ref[...]
    vb = vn.astype(BF16)
    for c in range(nchunk):
        rows = slice(c * BLK, (c + 1) * BLK)
        for h in range(H_A):
            cols = slice(h * DH_A, (h + 1) * DH_A)
            mix = _dot(ws_ref[h], vb[rows, cols]) + bias_ref[:, cols]
            u = _gelu(u_ref[rows, cols])
            o_ref[rows, cols] = (u * mix).astype(o_ref.dtype)


def _gmlp(proj, ws, bias, ln_g, ln_b, *, tr):
    m = proj.shape[0]
    tr = min(tr, m)
    return pl.pallas_call(
        functools.partial(_gmlp_body, nchunk=tr // BLK),
        grid=(m // tr,),
        in_specs=[pl.BlockSpec((tr, D_A), lambda i: (i, 0)),
                  pl.BlockSpec((tr, D_A), lambda i: (i, 1)),
                  pl.BlockSpec((1, D_A), lambda i: (0, 0)),
                  pl.BlockSpec((1, D_A), lambda i: (0, 0)),
                  pl.BlockSpec((H_A, BLK, BLK), lambda i: (0, 0, 0)),
                  pl.BlockSpec((BLK, D_A), lambda i: (0, 0))],
        out_specs=pl.BlockSpec((tr, D_A), lambda i: (i, 0)),
        out_shape=jax.ShapeDtypeStruct((m, D_A), BF16),
        compiler_params=_params("parallel"),
        name="gmlp_gate",
    )(proj, proj, ln_g.reshape(1, D_A), ln_b.reshape(1, D_A), ws, bias)


def _conv_body(x_ref, w_ref, b_ref, o_ref, *, seq):
    x = x_ref[...]
    row = lax.broadcasted_iota(jnp.int32, x.shape, 0)
    acc = x * w_ref[CONV_W // 2:CONV_W // 2 + 1, :] + b_ref[...]
    for k in range(CONV_W):
        d = k - CONV_W // 2
        if d == 0:
            continue
        shifted = pltpu.roll(x, (-d) % seq, 0)
        valid = (row >= -d) if d < 0 else (row < seq - d)
        acc = acc + jnp.where(valid, shifted, 0.0) * w_ref[k:k + 1, :]
    o_ref[...] = _silu(acc).astype(o_ref.dtype)


def _conv_silu(proj, w, b, *, nb, seq):
    tc = 512
    col0 = (AB_MAIN - CONV_DIM) // tc
    return pl.pallas_call(
        functools.partial(_conv_body, seq=seq),
        grid=(nb, CONV_DIM // tc),
        in_specs=[pl.BlockSpec((seq, tc), lambda b_, j: (b_, col0 + j)),
                  pl.BlockSpec((CONV_W, tc), lambda b_, j: (0, j)),
                  pl.BlockSpec((1, tc), lambda b_, j: (0, j))],
        out_specs=pl.BlockSpec((seq, tc), lambda b_, j: (b_, j)),
        out_shape=jax.ShapeDtypeStruct((nb * seq, CONV_DIM), BF16),
        compiler_params=_params("parallel", "parallel"),
        name="conv_silu",
    )(proj, w, b.reshape(1, CONV_DIM))


def _split3(v):
    hi = v.astype(BF16)
    r1 = v - hi.astype(F32)
    mid = r1.astype(BF16)
    lo = (r1 - mid.astype(F32)).astype(BF16)
    return jnp.concatenate([hi, mid, lo], axis=1)


def _ssd_body(*refs, direction, nc, epilogue):
    if epilogue:
        (xs_ref, b_ref, c_ref, dt_ref, dtt_ref, biasc_ref, alogc_ref, biasr_ref, alogr_ref, xexp_ref, h0_ref,
         yf_ref, z_ref, dskip_ref, gn_ref, y_ref, hout_ref, h_scr, y_scr) = refs
    else:
        (xs_ref, b_ref, c_ref, dt_ref, dtt_ref, biasc_ref, alogc_ref, biasr_ref, alogr_ref, xexp_ref, h0_ref,
         y_ref, hout_ref, h_scr, y_scr) = refs
    step = pl.program_id(1)

    @pl.when(step == 0)
    def _():
        h_scr[...] = h0_ref[0]

    off = direction * H_B
    dt_c = _softplus(dt_ref[...] + biasc_ref[...])
    dta_c = dt_c * (-jnp.exp(alogc_ref[...]))
    dt_r = _softplus(dtt_ref[off:off + H_B, :] + biasr_ref[...])
    dta_r = dt_r * (-jnp.exp(alogr_ref[...]))

    ii = lax.broadcasted_iota(jnp.int32, (BLK, BLK), 0)
    jj = lax.broadcasted_iota(jnp.int32, (BLK, BLK), 1)
    tri = (jj <= ii) if direction == 0 else (jj >= ii)
    tri_f = tri.astype(F32)
    s_c = jnp.dot(tri_f, dta_c, precision=lax.Precision.HIGHEST, preferred_element_type=F32)
    s_r = lax.dot_general(dta_r, tri_f, (((1,), (1,)), ((), ())), precision=lax.Precision.HIGHEST,
                          preferred_element_type=F32)
    last = BLK - 1 if direction == 0 else 0
    tot_c = s_c[last:last + 1, :]

    lane_c = lax.broadcasted_iota(jnp.int32, (BLK, DT_PAD), 1)
    mine = (lane_c >= off) & (lane_c < off + H_B)
    w_c = jnp.where(mine, jnp.exp(tot_c - s_c) * dt_c, 0.0)
    w_exp = _dot(_split3(w_c), xexp_ref[...])
    lane_8 = lax.broadcasted_iota(jnp.int32, (8, DT_PAD), 1)
    dec_c = jnp.where((lane_8 >= off) & (lane_8 < off + H_B), jnp.broadcast_to(jnp.exp(tot_c), (8, DT_PAD)), 0.0)
    dec_exp = _dot(_split3(dec_c), xexp_ref[...])[0:1, :]

    xs = xs_ref[...]
    h_in = h_scr[...]
    h_in_b = h_in.astype(BF16)
    lane = lax.broadcasted_iota(jnp.int32, (BLK, BLK), 1)
    first_head = lane < HEAD_P

    for g in range(G_B):
        cg = c_ref[:, g * N_STATE:(g + 1) * N_STATE]
        bg = b_ref[:, g * N_STATE:(g + 1) * N_STATE]
        cb = _dot_nt(cg, bg)
        cg_f = cg.astype(F32)
        for pair in range(R_B // 2):
            cols = slice((g * (R_B // 2) + pair) * 2 * HEAD_P, (g * (R_B // 2) + pair + 1) * 2 * HEAD_P)
            rhs = jnp.concatenate([xs[:, cols], h_in_b[:, cols]], axis=0)
            outs = []
            for q in range(2):
                r = g * R_B + pair * 2 + q
                s_col = jnp.broadcast_to(s_c[:, off + r:off + r + 1], (BLK, BLK))
                seg = s_col - s_r[r:r + 1, :]
                decay = jnp.exp(jnp.where(tri, seg, -jnp.inf))
                m_in = (cb * decay * dt_r[r:r + 1, :]).astype(BF16)
                m_st = (cg_f * jnp.exp(s_col)).astype(BF16)
                outs.append(_dot(jnp.concatenate([m_in, m_st], axis=1), rhs))
            y_scr[:, cols] = jnp.where(first_head, outs[0], outs[1])

    xw = (xs.astype(F32) * w_exp).astype(BF16)
    for g in range(G_B):
        cols = slice(g * R_B * HEAD_P, (g + 1) * R_B * HEAD_P)
        bg = b_ref[:, g * N_STATE:(g + 1) * N_STATE]
        h_scr[:, cols] = h_in[:, cols] * dec_exp[:, cols] + _dot_tn(bg, xw[:, cols])

    @pl.when(step == nc - 1)
    def _():
        hout_ref[0] = h_scr[...]

    if epilogue:
        y = y_scr[...] + yf_ref[...] + dskip_ref[...] * xs.astype(F32)
        y = y * _silu(z_ref[...])
        for g in range(G_B):
            cols = slice(g * R_B * HEAD_P, (g + 1) * R_B * HEAD_P)
            y_ref[:, cols] = _rms(y[:, cols], gn_ref[:, cols]).astype(y_ref.dtype)
    else:
        y_ref[...] = y_scr[...]


def _ssd(xbc, dt, dtt, consts, h0, *, nb, seq, direction, epi=None):
    nc = seq // BLK
    m = nb * seq
    bias_c, alog_c, bias_r, alog_r, xexp = consts[direction]

    def tok(b_, c_):
        return b_ * nc + (c_ if direction == 0 else nc - 1 - c_)

    const2 = lambda b_, c_: (0, 0)
    in_specs = [pl.BlockSpec((BLK, D_INNER), lambda b_, c_: (tok(b_, c_), 0)),
                pl.BlockSpec((BLK, G_B * N_STATE), lambda b_, c_: (tok(b_, c_), D_INNER // (G_B * N_STATE))),
                pl.BlockSpec((BLK, G_B * N_STATE), lambda b_, c_: (tok(b_, c_), D_INNER // (G_B * N_STATE) + 1)),
                pl.BlockSpec((BLK, DT_PAD), lambda b_, c_: (tok(b_, c_), 0)),
                pl.BlockSpec((2 * H_B, BLK), lambda b_, c_: (0, tok(b_, c_))),
                pl.BlockSpec((1, DT_PAD), const2),
                pl.BlockSpec((1, DT_PAD), const2),
                pl.BlockSpec((H_B, BLK), const2),
                pl.BlockSpec((H_B, BLK), const2),
                pl.BlockSpec((3 * DT_PAD, D_INNER), const2),
                pl.BlockSpec((1, N_STATE, D_INNER), lambda b_, c_: (b_, 0, 0))]
    args = [xbc, xbc, xbc, dt, dtt, bias_c, alog_c, bias_r, alog_r, xexp, h0]
    if epi is not None:
        y_fwd, proj, d_skip, gn_g = epi
        in_specs += [pl.BlockSpec((BLK, D_INNER), lambda b_, c_: (tok(b_, c_), 0)),
                     pl.BlockSpec((BLK, D_INNER), lambda b_, c_: (tok(b_, c_), 2 * D_A // D_INNER)),
                     pl.BlockSpec((1, D_INNER), const2),
                     pl.BlockSpec((1, D_INNER), const2)]
        args += [y_fwd, proj, d_skip, gn_g]
    return pl.pallas_call(
        functools.partial(_ssd_body, direction=direction, nc=nc, epilogue=epi is not None),
        grid=(nb, nc),
        in_specs=in_specs,
        out_specs=[pl.BlockSpec((BLK, D_INNER), lambda b_, c_: (tok(b_, c_), 0)),
                   pl.BlockSpec((1, N_STATE, D_INNER), lambda b_, c_: (b_, 0, 0))],
        out_shape=[jax.ShapeDtypeStruct((m, D_INNER), BF16 if epi is not None else F32),
                   jax.ShapeDtypeStruct((nb, N_STATE, D_INNER), F32)],
        scratch_shapes=[pltpu.VMEM((N_STATE, D_INNER), F32), pltpu.VMEM((BLK, D_INNER), F32)],
        compiler_params=_params("parallel", "arbitrary"),
        name="ssd_bwd_gate" if epi is not None else "ssd_fwd",
    )(*args)


def _proj_resid_body(a1_ref, a2_ref, w_ref, h_ref, mod_ref, g_ref, o_ref, acc_ref, *, nk1, nk, gate_idx):
    k = pl.program_id(1)

    @pl.when(k == 0)
    def _():
        acc_ref[...] = jnp.zeros_like(acc_ref)

    @pl.when(k < nk1)
    def _():
        acc_ref[...] += _dot(a1_ref[...], w_ref[...])

    @pl.when(k >= nk1)
    def _():
        acc_ref[...] += _dot(a2_ref[...], w_ref[...])

    @pl.when(k == nk - 1)
    def _():
        o_ref[...] = h_ref[...] + mod_ref[0, gate_idx:gate_idx + 1, :] * _rms(acc_ref[...], g_ref[...])


def _proj_resid(a1, a2, w, h, mods, g, *, gate_idx, tm, tk):
    m, k1 = a1.shape
    n = w.shape[1]
    tm = min(tm, m // mods.shape[0])
    tk = min(tk, k1)
    nk1 = k1 // tk
    nk = 2 * nk1
    rows_per_mod = m // mods.shape[0]
    assert rows_per_mod % tm == 0 and a2.shape == a1.shape and w.shape[0] == 2 * k1
    return pl.pallas_call(
        functools.partial(_proj_resid_body, nk1=nk1, nk=nk, gate_idx=gate_idx),
        grid=(m // tm, nk),
        in_specs=[pl.BlockSpec((tm, tk), lambda i, k: (i, jnp.minimum(k, nk1 - 1))),
                  pl.BlockSpec((tm, tk), lambda i, k: (i, jnp.maximum(k - nk1, 0))),
                  pl.BlockSpec((tk, n), lambda i, k: (k, 0)),
                  pl.BlockSpec((tm, n), lambda i, k: (i, 0)),
                  pl.BlockSpec((1, N_MOD, n), lambda i, k: ((i * tm) // rows_per_mod, 0, 0)),
                  pl.BlockSpec((1, n), lambda i, k: (0, 0))],
        out_specs=pl.BlockSpec((tm, n), lambda i, k: (i, 0)),
        out_shape=jax.ShapeDtypeStruct((m, n), F32),
        scratch_shapes=[pltpu.VMEM((tm, n), F32)],
        compiler_params=_params("parallel", "arbitrary"),
        name="proj_resid",
    )(a1, a2, w, h, mods, g.reshape(1, n))


def _mlp_body(h_ref, mod_ref, gpre_ref, gpost_ref, w1_ref, w2_ref, o_ref, a_ref, acc_ref, *, nf):
    f = pl.program_id(1)

    @pl.when(f == 0)
    def _():
        a = _rms(h_ref[...], gpre_ref[...]) * (1.0 + mod_ref[0, 4:5, :]) + mod_ref[0, 3:4, :]
        a_ref[...] = a.astype(BF16)
        acc_ref[...] = jnp.zeros_like(acc_ref)

    hid = jnp.square(jnp.maximum(_dot(a_ref[...], w1_ref[...]), 0.0)).astype(BF16)
    acc_ref[...] += _dot(hid, w2_ref[...])

    @pl.when(f == nf - 1)
    def _():
        o_ref[...] = h_ref[...] + mod_ref[0, 5:6, :] * _rms(acc_ref[...], gpost_ref[...])


def _mlp(h, mods, g_pre, g_post, w1, w2, *, tm, tf):
    m, d = h.shape
    ff = w1.shape[1]
    tm = min(tm, m // mods.shape[0])
    rows_per_mod = m // mods.shape[0]
    assert rows_per_mod % tm == 0 and ff % tf == 0
    nf = ff // tf
    return pl.pallas_call(
        functools.partial(_mlp_body, nf=nf),
        grid=(m // tm, nf),
        in_specs=[pl.BlockSpec((tm, d), lambda i, f: (i, 0)),
                  pl.BlockSpec((1, N_MOD, d), lambda i, f: ((i * tm) // rows_per_mod, 0, 0)),
                  pl.BlockSpec((1, d), lambda i, f: (0, 0)),
                  pl.BlockSpec((1, d), lambda i, f: (0, 0)),
                  pl.BlockSpec((d, tf), lambda i, f: (0, f)),
                  pl.BlockSpec((tf, d), lambda i, f: (f, 0))],
        out_specs=pl.BlockSpec((tm, d), lambda i, f: (i, 0)),
        out_shape=jax.ShapeDtypeStruct((m, d), F32),
        scratch_shapes=[pltpu.VMEM((tm, d), BF16), pltpu.VMEM((tm, d), F32)],
        compiler_params=_params("parallel", "arbitrary"),
        name="mlp_sublayer",
    )(h, mods, g_pre.reshape(1, d), g_post.reshape(1, d), w1, w2)


_QC_HEADS = range(0, H_C)
_KC_HEADS = range(H_C, H_C + KV_C)
_QD_HEADS = range(H_C + 2 * KV_C, H_C + 2 * KV_C + H_D)
_KD_HEADS = range(H_C + 2 * KV_C + H_D, H_C + 2 * KV_C + H_D + KV_D)


def _qkv_prep_body(x_ref, cos_ref, sin_ref, qg_ref, kg_ref, o_ref, *, rope):
    lane = lax.broadcasted_iota(jnp.int32, (x_ref.shape[0], HEAD_DIM), 1)
    first_half = (lane % AXIS_DIM) < (AXIS_DIM // 2)
    for hd in range(CD_IN // HEAD_DIM):
        cols = slice(hd * HEAD_DIM, (hd + 1) * HEAD_DIM)
        x = x_ref[:, cols]
        if hd in _QC_HEADS:
            x = _rms(x, qg_ref[...])
        elif hd in _KC_HEADS:
            x = _rms(x, kg_ref[...])
        if rope and (hd in _QC_HEADS or hd in _KC_HEADS or hd in _QD_HEADS or hd in _KD_HEADS):
            partner = jnp.where(first_half,
                                pltpu.roll(x, HEAD_DIM - AXIS_DIM // 2, 1),
                                pltpu.roll(x, AXIS_DIM // 2, 1))
            x = x * cos_ref[...] + partner * sin_ref[...]
        o_ref[:, cols] = x.astype(o_ref.dtype)


def _qkv_prep(proj, cos, sin_signed, q_g, k_g, *, seq, rope, tr):
    m = proj.shape[0]
    tr = min(tr, seq)
    per_seq = seq // tr
    return pl.pallas_call(
        functools.partial(_qkv_prep_body, rope=rope),
        grid=(m // tr,),
        in_specs=[pl.BlockSpec((tr, CD_IN), lambda i: (i, 0)),
                  pl.BlockSpec((tr, HEAD_DIM), lambda i: (i % per_seq, 0)),
                  pl.BlockSpec((tr, HEAD_DIM), lambda i: (i % per_seq, 0)),
                  pl.BlockSpec((1, HEAD_DIM), lambda i: (0, 0)),
                  pl.BlockSpec((1, HEAD_DIM), lambda i: (0, 0))],
        out_specs=pl.BlockSpec((tr, CD_IN), lambda i: (i, 0)),
        out_shape=jax.ShapeDtypeStruct((m, CD_IN), BF16),
        compiler_params=_params("parallel"),
        name="qkv_prep",
    )(proj, cos, sin_signed, q_g.reshape(1, HEAD_DIM), k_g.reshape(1, HEAD_DIM))


def _attn_c_body(q_ref, k_ref, v_ref, kc_ref, vc_ref, o_ref):
    k = k_ref[...]
    v = v_ref[...]
    kc = kc_ref[...]
    vc = vc_ref[...]
    for r in range(H_C // KV_C):
        cols = slice(r * HEAD_DIM, (r + 1) * HEAD_DIM)
        q = q_ref[:, cols]
        s1 = _dot_nt(q, k)
        s2 = _dot_nt(q, kc)
        mx = jnp.maximum(jnp.max(s1, axis=-1, keepdims=True), jnp.max(s2, axis=-1, keepdims=True))
        p1 = jnp.exp((s1 - mx) * ATTN_SCALE)
        p2 = jnp.exp((s2 - mx) * ATTN_SCALE)
        den = jnp.sum(p1, axis=-1, keepdims=True) + jnp.sum(p2, axis=-1, keepdims=True)
        o = _dot(p1.astype(BF16), v) + _dot(p2.astype(BF16), vc)
        o_ref[:, cols] = (o / den).astype(o_ref.dtype)


def _attn_c(qkv, qkv_ctx, *, nb, seq, ctx_len, tq):
    tq = min(tq, seq)
    nq = seq // tq
    rq = H_C // KV_C
    k0 = H_C
    v0 = H_C + KV_C
    return pl.pallas_call(
        _attn_c_body,
        grid=(nb, KV_C, nq),
        in_specs=[pl.BlockSpec((tq, rq * HEAD_DIM), lambda b_, h, i: (b_ * nq + i, h)),
                  pl.BlockSpec((seq, HEAD_DIM), lambda b_, h, i: (b_, k0 + h)),
                  pl.BlockSpec((seq, HEAD_DIM), lambda b_, h, i: (b_, v0 + h)),
                  pl.BlockSpec((ctx_len, HEAD_DIM), lambda b_, h, i: (b_, k0 + h)),
                  pl.BlockSpec((ctx_len, HEAD_DIM), lambda b_, h, i: (b_, v0 + h))],
        out_specs=pl.BlockSpec((tq, rq * HEAD_DIM), lambda b_, h, i: (b_ * nq + i, h)),
        out_shape=jax.ShapeDtypeStruct((nb * seq, H_C * HEAD_DIM), BF16),
        compiler_params=_params("parallel", "parallel", "arbitrary"),
        name="attn_full",
    )(qkv, qkv, qkv, qkv_ctx, qkv_ctx)


def _attn_d_body(q_ref, kp_ref, kn_ref, kx_ref, vp_ref, vn_ref, vx_ref, kc_ref, vc_ref, sink_ref, o_ref, *, nblk):
    n = pl.program_id(2)
    ii = lax.broadcasted_iota(jnp.int32, (BLK, BLK), 0)
    jj = lax.broadcasted_iota(jnp.int32, (BLK, BLK), 1)
    prev_ok = (jj >= ii) & (n > 0)
    next_ok = (jj <= ii) & (n < nblk - 1)
    neg = -jnp.inf
    for r in range(H_D // KV_D):
        cols = slice(r * HEAD_DIM, (r + 1) * HEAD_DIM)
        q = q_ref[:, cols]
        sp = jnp.where(prev_ok, _dot_nt(q, kp_ref[...]) * ATTN_SCALE, neg)
        sn = _dot_nt(q, kn_ref[...]) * ATTN_SCALE
        sx = jnp.where(next_ok, _dot_nt(q, kx_ref[...]) * ATTN_SCALE, neg)
        sc = _dot_nt(q, kc_ref[...]) * ATTN_SCALE
        sk = sink_ref[0, r:r + 1, :]
        mx = jnp.maximum(jnp.maximum(jnp.max(sp, axis=-1, keepdims=True), jnp.max(sn, axis=-1, keepdims=True)),
                         jnp.maximum(jnp.max(sx, axis=-1, keepdims=True), jnp.max(sc, axis=-1, keepdims=True)))
        mx = jnp.maximum(mx, sk[:, 0:1])
        pp = jnp.exp(sp - mx)
        pn = jnp.exp(sn - mx)
        px = jnp.exp(sx - mx)
        pc = jnp.exp(sc - mx)
        den = (jnp.sum(pp, axis=-1, keepdims=True) + jnp.sum(pn, axis=-1, keepdims=True)
               + jnp.sum(px, axis=-1, keepdims=True) + jnp.sum(pc, axis=-1, keepdims=True)
               + jnp.exp(sk[:, 0:1] - mx))
        o = (_dot(pp.astype(BF16), vp_ref[...]) + _dot(pn.astype(BF16), vn_ref[...])
             + _dot(px.astype(BF16), vx_ref[...]) + _dot(pc.astype(BF16), vc_ref[...]))
        o_ref[:, cols] = (o / den).astype(o_ref.dtype)


def _attn_d(qkv, qkv_ctx, sink, *, nb, seq, ctx_len):
    nblk = seq // BLK
    rq = H_D // KV_D
    q0 = (H_C + 2 * KV_C) // rq
    k0 = H_C + 2 * KV_C + H_D
    v0 = k0 + KV_D

    def blk(col0, shift):
        def index(b_, h, n):
            return (b_ * nblk + jnp.clip(n + shift, 0, nblk - 1), col0 + h)
        return pl.BlockSpec((BLK, HEAD_DIM), index)

    return pl.pallas_call(
        functools.partial(_attn_d_body, nblk=nblk),
        grid=(nb, KV_D, nblk),
        in_specs=[pl.BlockSpec((BLK, rq * HEAD_DIM), lambda b_, h, n: (b_ * nblk + n, q0 + h)),
                  blk(k0, -1), blk(k0, 0), blk(k0, 1),
                  blk(v0, -1), blk(v0, 0), blk(v0, 1),
                  pl.BlockSpec((ctx_len, HEAD_DIM), lambda b_, h, n: (b_, k0 + h)),
                  pl.BlockSpec((ctx_len, HEAD_DIM), lambda b_, h, n: (b_, v0 + h)),
                  pl.BlockSpec((1, rq, BLK), lambda b_, h, n: (h, 0, 0))],
        out_specs=pl.BlockSpec((BLK, rq * HEAD_DIM), lambda b_, h, n: (b_ * nblk + n, h)),
        out_shape=jax.ShapeDtypeStruct((nb * seq, H_D * HEAD_DIM), BF16),
        compiler_params=_params("parallel", "parallel", "arbitrary"),
        name="attn_window",
    )(qkv, qkv, qkv, qkv, qkv, qkv, qkv, qkv_ctx, qkv_ctx, sink)


def _rope_tables(seq):
    t = jnp.arange(seq)
    pos = jnp.stack([t // GRID_W, t % GRID_W], axis=-1).astype(F32)
    inv_freq = ROPE_BASE ** (-jnp.arange(0, AXIS_DIM, 2, dtype=F32) / AXIS_DIM)
    ang = pos[:, :, None] * inv_freq
    cos, sin = jnp.cos(ang), jnp.sin(ang)
    cos_t = jnp.concatenate([cos[:, 0], cos[:, 0], cos[:, 1], cos[:, 1]], axis=-1)
    sin_t = jnp.concatenate([-sin[:, 0], sin[:, 0], -sin[:, 1], sin[:, 1]], axis=-1)
    return cos_t, sin_t


def _ssd_consts(dt_bias, a_log):
    out = []
    head_of_lane = jnp.arange(D_INNER) // HEAD_P
    for direction in range(2):
        pad = lambda v: jnp.zeros((1, DT_PAD), F32).at[0, :2 * H_B].set(v.reshape(-1))
        rows = jnp.arange(DT_PAD)[:, None] - direction * H_B
        sel = (rows == head_of_lane[None, :]).astype(BF16)
        out.append((pad(dt_bias), pad(a_log),
                    jnp.broadcast_to(dt_bias[direction][:, None], (H_B, BLK)),
                    jnp.broadcast_to(a_log[direction][:, None], (H_B, BLK)),
                    jnp.concatenate([sel, sel, sel], axis=0)))
    return out


def _ab_layer(h_x, h_c, mods_x, mods_c, norm_g, w_in, w_s, b_s, ln_g, ln_b, conv_w, conv_b, a_log, dt_bias,
              d_skip, gn_g, w_out, w1, w2, *, nb, seq, ctx_len):
    w_main = w_in[:, :AB_MAIN].astype(BF16)
    w_dt = jnp.zeros((D_MODEL, DT_PAD), F32).at[:, :2 * H_B].set(w_in[:, AB_MAIN:]).astype(BF16)
    ws = w_s.astype(BF16)
    bias = jnp.repeat(b_s.T, DH_A, axis=1)
    consts = _ssd_consts(dt_bias, a_log)
    dskip = jnp.repeat(d_skip, HEAD_P).reshape(1, D_INNER)
    gn = gn_g.reshape(1, D_INNER)
    w_out_b = w_out.astype(BF16)
    w1_b = w1.astype(BF16)
    w2_b = w2.astype(BF16)

    def project(h, mods):
        proj = _nmm(h, norm_g[0], mods, w_main, shift_idx=0, scale_idx=1, out_dtype=F32, tm=1024, tn=1024)
        dt = _nmm(h, norm_g[0], mods, w_dt, shift_idx=0, scale_idx=1, out_dtype=F32, tm=1024, tn=DT_PAD)
        return proj, dt, dt[:, :2 * H_B].T

    def scans(proj, dt, dtt, n_tok, h0):
        xbc = _conv_silu(proj, conv_w, conv_b, nb=nb, seq=n_tok)
        y_f, hf = _ssd(xbc, dt, dtt, consts, h0[0], nb=nb, seq=n_tok, direction=0)
        y_b, hb = _ssd(xbc, dt, dtt, consts, h0[1], nb=nb, seq=n_tok, direction=1, epi=(y_f, proj, dskip, gn))
        return y_b, (hf, hb)

    def finish(h, mods, ya, yb):
        h = _proj_resid(ya, yb, w_out_b, h, mods, norm_g[1], gate_idx=2, tm=512, tk=1024)
        return _mlp(h, mods, norm_g[2], norm_g[3], w1_b, w2_b, tm=512, tf=512)

    proj_c, dt_c, dtt_c = project(h_c, mods_c)
    proj_x, dt_x, dtt_x = project(h_x, mods_x)
    zero = jnp.zeros((nb, N_STATE, D_INNER), F32)
    yb_c, st_c = scans(proj_c, dt_c, dtt_c, ctx_len, (zero, zero))
    yb_x, _ = scans(proj_x, dt_x, dtt_x, seq, st_c)
    ya_c = _gmlp(proj_c, ws, bias, ln_g, ln_b, tr=512)
    ya_x = _gmlp(proj_x, ws, bias, ln_g, ln_b, tr=512)
    return finish(h_x, mods_x, ya_x, yb_x), finish(h_c, mods_c, ya_c, yb_c)


def _cd_layer(h_x, h_c, mods_x, mods_c, norm_g, w_in, q_g, k_g, sink, w_out, w1, w2, *, nb, seq, ctx_len):
    w_in_b = w_in.astype(BF16)
    cos_t, sin_t = _rope_tables(seq)
    proj_x = _nmm(h_x, norm_g[0], mods_x, w_in_b, shift_idx=0, scale_idx=1, out_dtype=F32, tm=1024, tn=1024)
    proj_c = _nmm(h_c, norm_g[0], mods_c, w_in_b, shift_idx=0, scale_idx=1, out_dtype=F32, tm=1024, tn=1024)
    qkv_x = _qkv_prep(proj_x, cos_t, sin_t, q_g, k_g, seq=seq, rope=True, tr=512)
    qkv_c = _qkv_prep(proj_c, cos_t, sin_t, q_g, k_g, seq=ctx_len, rope=False, tr=256)
    oc = _attn_c(qkv_x, qkv_c, nb=nb, seq=seq, ctx_len=ctx_len, tq=512)
    sink_t = jnp.broadcast_to(sink.reshape(KV_D, H_D // KV_D, 1), (KV_D, H_D // KV_D, BLK))
    od = _attn_d(qkv_x, qkv_c, sink_t, nb=nb, seq=seq, ctx_len=ctx_len)
    h = _proj_resid(oc, od, w_out.astype(BF16), h_x, mods_x, norm_g[1], gate_idx=2, tm=512, tk=1024)
    return _mlp(h, mods_x, norm_g[2], norm_g[3], w1.astype(BF16), w2.astype(BF16), tm=512, tf=512)


def kernel(x, c, ctx, c_ctx, mod_w, mod_b, norm_g, mlp_w1, mlp_w2, ab_w_in, a_w_s, a_b_s, a_ln_g, a_ln_b,
           b_conv_w, b_conv_b, b_a_log, b_dt_bias, b_d, b_norm_g, ab_w_out, cd_w_in, c_q_norm_g, c_k_norm_g,
           d_sink, cd_w_out):
    nb, seq, d = x.shape
    ctx_len = ctx.shape[1]
    depth = mod_w.shape[0]
    assert depth == 2, "the odd layer is implemented as the last layer (no context update)"
    h_x = x.reshape(nb * seq, d)
    h_c = ctx.reshape(nb * ctx_len, d)
    cond_rows = -(-(nb + 1) // 8) * 8
    cond = jnp.zeros((cond_rows, d), F32).at[:nb].set(c).at[nb].set(c_ctx)
    for i in range(depth):
        j = i // 2
        mods = _modvec(cond, mod_w[i], mod_b[i])
        mods_x = mods[:nb].reshape(nb, N_MOD, d)
        mods_c = mods[nb:nb + 1].reshape(1, N_MOD, d)
        if i % 2 == 0:
            h_x, h_c = _ab_layer(h_x, h_c, mods_x, mods_c, norm_g[i], ab_w_in[j], a_w_s[j], a_b_s[j], a_ln_g[j],
                                 a_ln_b[j], b_conv_w[j], b_conv_b[j], b_a_log[j], b_dt_bias[j], b_d[j],
                                 b_norm_g[j], ab_w_out[j], mlp_w1[i], mlp_w2[i], nb=nb, seq=seq, ctx_len=ctx_len)
        else:
            h_x = _cd_layer(h_x, h_c, mods_x, mods_c, norm_g[i], cd_w_in[j], c_q_norm_g[j], c_k_norm_g[j],
                            d_sink[j], cd_w_out[j], mlp_w1[i], mlp_w2[i], nb=nb, seq=seq, ctx_len=ctx_len)
    return h_x.reshape(nb, seq, d)
```

```python
import functools
import math

import jax
import jax.numpy as jnp
from jax import lax
from jax.experimental import pallas as pl
from jax.experimental.pallas import tpu as pltpu

F32 = jnp.float32
BF16 = jnp.bfloat16

D_MODEL = 2048
GRID_W = 64
BLK = 128
EPS = 1e-6
N_MOD = 6

D_A = 2048
H_A = 8
DH_A = D_A // H_A

D_INNER = 2048
HEAD_P = 64
H_B = D_INNER // HEAD_P
G_B = 4
R_B = H_B // G_B
N_STATE = 128
CONV_W = 5
CONV_DIM = D_INNER + 2 * G_B * N_STATE

HEAD_DIM = 128
H_C = 8
KV_C = 2
H_D = 8
KV_D = 2
ROPE_BASE = 10000.0
AXIS_DIM = HEAD_DIM // 2
D_FF = 4 * D_MODEL
CD_IN = (H_C + 2 * KV_C + H_D + 2 * KV_D) * HEAD_DIM
ATTN_SCALE = HEAD_DIM ** -0.5
LOG2_E = math.log2(math.e)

AB_MAIN = 2 * D_A + D_INNER + CONV_DIM
DT_PAD = 128

VMEM_LIMIT_V7X = 56 * 1024 * 1024


def _params(*sem, vmem=VMEM_LIMIT_V7X):
    return pltpu.CompilerParams(dimension_semantics=sem, vmem_limit_bytes=vmem)


def _rms(x, g):
    return x * lax.rsqrt(jnp.mean(x * x, axis=-1, keepdims=True) + EPS) * g


def _silu(x):
    return x * jax.nn.sigmoid(x)


def _gelu(x):
    return 0.5 * x * (1.0 + lax.erf(x * (1.0 / math.sqrt(2.0))))


def _softplus(x):
    return jnp.maximum(x, 0.0) + jnp.log1p(jnp.exp(-jnp.abs(x)))


def _dot(a, b):
    return jnp.dot(a, b, preferred_element_type=F32)


def _dot_nt(a, b):
    return lax.dot_general(a, b, (((1,), (1,)), ((), ())), preferred_element_type=F32)


def _dot_tn(a, b):
    return lax.dot_general(a, b, (((0,), (0,)), ((), ())), preferred_element_type=F32)


def _norm_modulate_rows(x_ref, g_ref, mod_ref, a_ref, shift_idx, scale_idx):
    gs = g_ref[...] * (1.0 + mod_ref[0, scale_idx:scale_idx + 1, :])
    sh = mod_ref[0, shift_idx:shift_idx + 1, :]

    def strip(i, carry):
        r = pl.ds(pl.multiple_of(i * 16, 16), 16)
        x = x_ref[r, :]
        rs = lax.rsqrt(jnp.mean(x * x, axis=-1, keepdims=True) + EPS)
        a_ref[r, :] = (x * rs * gs + sh).astype(a_ref.dtype)
        return carry

    lax.fori_loop(0, x_ref.shape[0] // 16, strip, 0, unroll=8)


def _gated_norm_residual_rows(y_ref, h_ref, g_ref, mod_ref, gate_idx, o_ref):
    gg = mod_ref[0, gate_idx:gate_idx + 1, :] * g_ref[...]

    def strip(i, carry):
        r = pl.ds(pl.multiple_of(i * 8, 8), 8)
        y = y_ref[r, :]
        rs = lax.rsqrt(jnp.mean(y * y, axis=-1, keepdims=True) + EPS)
        o_ref[r, :] = h_ref[r, :] + y * rs * gg
        return carry

    lax.fori_loop(0, y_ref.shape[0] // 8, strip, 0, unroll=16)


def _modvec_body(s_ref, w_ref, b_ref, o_ref):
    a = _silu(s_ref[...]).astype(BF16)
    o_ref[...] = _dot(a, w_ref[...].astype(BF16)) + b_ref[...]


def _modvec(s, w, b, layer):
    r, d = s.shape
    depth, _, n = w.shape
    tn = 1024
    return pl.pallas_call(
        _modvec_body,
        grid=(n // tn,),
        in_specs=[pl.BlockSpec((r, d), lambda j: (0, 0)),
                  pl.BlockSpec((None, d, tn), lambda j: (layer, 0, j)),
                  pl.BlockSpec((None, 1, tn), lambda j: (layer, 0, j))],
        out_specs=pl.BlockSpec((r, tn), lambda j: (0, j)),
        out_shape=jax.ShapeDtypeStruct((r, n), F32),
        compiler_params=_params("arbitrary"),
        name="modvec",
    )(s, w, b.reshape(depth, 1, n))


def _nmm_body(*refs, shift_idx, scale_idx, side):
    if side:
        x_ref, g_ref, mod_ref, w_ref, ws_ref, bs_ref, o_ref, os_ref, a_ref = refs
    else:
        x_ref, g_ref, mod_ref, w_ref, o_ref, a_ref = refs

    @pl.when(pl.program_id(1) == 0)
    def _():
        _norm_modulate_rows(x_ref, g_ref, mod_ref, a_ref, shift_idx, scale_idx)
        if side:
            os_ref[...] = _softplus(_dot(a_ref[...], ws_ref[...]) + bs_ref[...])

    o_ref[...] = _dot(a_ref[...], w_ref[...]).astype(o_ref.dtype)


def _nmm(x, g, mods, w, w_side=None, b_side=None, *, shift_idx, scale_idx, out_dtype, tm, tn):
    m, k = x.shape
    n = w.shape[1]
    tm = min(tm, m // mods.shape[0])
    tn = min(tn, n)
    rows_per_mod = m // mods.shape[0]
    assert rows_per_mod % tm == 0 and n % tn == 0 and m % tm == 0
    side = w_side is not None
    in_specs = [pl.BlockSpec((tm, k), lambda i, j: (i, 0)),
                pl.BlockSpec((1, k), lambda i, j: (0, 0)),
                pl.BlockSpec((1, N_MOD, k), lambda i, j: ((i * tm) // rows_per_mod, 0, 0)),
                pl.BlockSpec((k, tn), lambda i, j: (0, j))]
    out_specs = [pl.BlockSpec((tm, tn), lambda i, j: (i, j))]
    out_shape = [jax.ShapeDtypeStruct((m, n), out_dtype)]
    args = [x, g.reshape(1, k), mods, w]
    if side:
        ns = w_side.shape[1]
        in_specs += [pl.BlockSpec((k, ns), lambda i, j: (0, 0)), pl.BlockSpec((1, ns), lambda i, j: (0, 0))]
        out_specs.append(pl.BlockSpec((tm, ns), lambda i, j: (i, 0)))
        out_shape.append(jax.ShapeDtypeStruct((m, ns), F32))
        args += [w_side, b_side]
    out = pl.pallas_call(
        functools.partial(_nmm_body, shift_idx=shift_idx, scale_idx=scale_idx, side=side),
        grid=(m // tm, n // tn),
        in_specs=in_specs,
        out_specs=out_specs,
        out_shape=out_shape,
        scratch_shapes=[pltpu.VMEM((tm, k), BF16)],
        compiler_params=_params("parallel", "arbitrary"),
        name="norm_mod_matmul",
    )(*args)
    return out if side else out[0]


def _gmlp_body(u_ref, v_ref, lng_ref, lnb_ref, ws_ref, bias_ref, o_ref, *, nchunk):
    v = _gelu(v_ref[...])
    mu = jnp.mean(v, axis=-1, keepdims=True)
    vc = v - mu
    vn = vc * lax.rsqrt(jnp.mean(vc * vc, axis=-1, keepdims=True) + EPS) * lng_ref[...] + lnb_ref[...]
    vb = vn.astype(BF16)
    for c in range(nchunk):
        rows = slice(c * BLK, (c + 1) * BLK)
        for h in range(H_A):
            cols = slice(h * DH_A, (h + 1) * DH_A)
            mix = _dot(ws_ref[h], vb[rows, cols]) + bias_ref[:, cols]
            u = _gelu(u_ref[rows, cols])
            o_ref[rows, cols] = (u * mix).astype(o_ref.dtype)


def _gmlp(proj, ws, bias, ln_g, ln_b, *, tr):
    m = proj.shape[0]
    tr = min(tr, m)
    return pl.pallas_call(
        functools.partial(_gmlp_body, nchunk=tr // BLK),
        grid=(m // tr,),
        in_specs=[pl.BlockSpec((tr, D_A), lambda i: (i, 0)),
                  pl.BlockSpec((tr, D_A), lambda i: (i, 1)),
                  pl.BlockSpec((1, D_A), lambda i: (0, 0)),
                  pl.BlockSpec((1, D_A), lambda i: (0, 0)),
                  pl.BlockSpec((H_A, BLK, BLK), lambda i: (0, 0, 0)),
                  pl.BlockSpec((BLK, D_A), lambda i: (0, 0))],
        out_specs=pl.BlockSpec((tr, D_A), lambda i: (i, 0)),
        out_shape=jax.ShapeDtypeStruct((m, D_A), BF16),
        compiler_params=_params("parallel"),
        name="gmlp_gate",
    )(proj, proj, ln_g.reshape(1, D_A), ln_b.reshape(1, D_A), ws, bias)


def _conv_body(x_ref, w_ref, b_ref, o_ref, *, seq):
    x = x_ref[...]
    row = lax.broadcasted_iota(jnp.int32, x.shape, 0)
    acc = x * w_ref[CONV_W // 2:CONV_W // 2 + 1, :] + b_ref[...]
    for k in range(CONV_W):
        d = k - CONV_W // 2
        if d == 0:
            continue
        shifted = pltpu.roll(x, (-d) % seq, 0)
        valid = (row >= -d) if d < 0 else (row < seq - d)
        acc = acc + jnp.where(valid, shifted, 0.0) * w_ref[k:k + 1, :]
    o_ref[...] = _silu(acc).astype(o_ref.dtype)


def _conv_silu(proj, w, b, *, nb, seq):
    tc = 512
    col0 = (AB_MAIN - CONV_DIM) // tc
    return pl.pallas_call(
        functools.partial(_conv_body, seq=seq),
        grid=(nb, CONV_DIM // tc),
        in_specs=[pl.BlockSpec((seq, tc), lambda b_, j: (b_, col0 + j)),
                  pl.BlockSpec((CONV_W, tc), lambda b_, j: (0, j)),
                  pl.BlockSpec((1, tc), lambda b_, j: (0, j))],
        out_specs=pl.BlockSpec((seq, tc), lambda b_, j: (b_, j)),
        out_shape=jax.ShapeDtypeStruct((nb * seq, CONV_DIM), BF16),
        compiler_params=_params("parallel", "parallel"),
        name="conv_silu",
    )(proj, w, b.reshape(1, CONV_DIM))


def _split3(v):
    hi = v.astype(BF16)
    r1 = v - hi.astype(F32)
    mid = r1.astype(BF16)
    lo = (r1 - mid.astype(F32)).astype(BF16)
    return jnp.concatenate([hi, mid, lo], axis=1)


def _ssd_body(*refs, direction, nc, epilogue):
    if epilogue:
        (xs_ref, b_ref, c_ref, dt_ref, dtt_ref, alogc_ref, alogr_ref, xexp_ref, h0_ref,
         yf_ref, z_ref, dskip_ref, gn_ref, y_ref, hout_ref, h_scr, y_scr) = refs
    else:
        (xs_ref, b_ref, c_ref, dt_ref, dtt_ref, alogc_ref, alogr_ref, xexp_ref, h0_ref,
         y_ref, hout_ref, h_scr, y_scr) = refs
    step = pl.program_id(1)

    @pl.when(step == 0)
    def _():
        h_scr[...] = h0_ref[0]

    off = direction * H_B
    dt_c = dt_ref[...]
    dta_c = dt_c * (-jnp.exp(alogc_ref[...]))
    dt_r = dtt_ref[off:off + H_B, :]
    dta_r = dt_r * (-jnp.exp(alogr_ref[...]))

    ii = lax.broadcasted_iota(jnp.int32, (BLK, BLK), 0)
    jj = lax.broadcasted_iota(jnp.int32, (BLK, BLK), 1)
    tri = (jj <= ii) if direction == 0 else (jj >= ii)
    tri_f = tri.astype(F32)
    s_c = jnp.dot(tri_f, dta_c, precision=lax.Precision.HIGHEST, preferred_element_type=F32)
    s_r = lax.dot_general(dta_r, tri_f, (((1,), (1,)), ((), ())), precision=lax.Precision.HIGHEST,
                          preferred_element_type=F32)
    last = BLK - 1 if direction == 0 else 0
    tot_c = s_c[last:last + 1, :]

    lane_c = lax.broadcasted_iota(jnp.int32, (BLK, DT_PAD), 1)
    mine = (lane_c >= off) & (lane_c < off + H_B)
    w_c = jnp.where(mine, jnp.exp(tot_c - s_c) * dt_c, 0.0)
    w_exp = _dot(_split3(w_c), xexp_ref[...])
    lane_8 = lax.broadcasted_iota(jnp.int32, (8, DT_PAD), 1)
    dec_c = jnp.where((lane_8 >= off) & (lane_8 < off + H_B), jnp.broadcast_to(jnp.exp(tot_c), (8, DT_PAD)), 0.0)
    dec_exp = _dot(_split3(dec_c), xexp_ref[...])[0:1, :]

    xs = xs_ref[...]
    h_in = h_scr[...]
    h_in_b = h_in.astype(BF16)
    lane = lax.broadcasted_iota(jnp.int32, (BLK, BLK), 1)
    first_head = lane < HEAD_P

    for g in range(G_B):
        cg = c_ref[:, g * N_STATE:(g + 1) * N_STATE]
        bg = b_ref[:, g * N_STATE:(g + 1) * N_STATE]
        cb = _dot_nt(cg, bg)
        cg_f = cg.astype(F32)
        for pair in range(R_B // 2):
            cols = slice((g * (R_B // 2) + pair) * 2 * HEAD_P, (g * (R_B // 2) + pair + 1) * 2 * HEAD_P)
            rhs = jnp.concatenate([xs[:, cols], h_in_b[:, cols]], axis=0)
            outs = []
            for q in range(2):
                r = g * R_B + pair * 2 + q
                s_col = jnp.broadcast_to(s_c[:, off + r:off + r + 1], (BLK, BLK))
                seg = s_col - s_r[r:r + 1, :]
                decay = jnp.exp(jnp.where(tri, seg, -jnp.inf))
                m_in = (cb * decay * dt_r[r:r + 1, :]).astype(BF16)
                m_st = (cg_f * jnp.exp(s_col)).astype(BF16)
                outs.append(_dot(jnp.concatenate([m_in, m_st], axis=1), rhs))
            y_scr[:, cols] = jnp.where(first_head, outs[0], outs[1])

    xw = (xs.astype(F32) * w_exp).astype(BF16)
    for g in range(G_B):
        cols = slice(g * R_B * HEAD_P, (g + 1) * R_B * HEAD_P)
        bg = b_ref[:, g * N_STATE:(g + 1) * N_STATE]
        h_scr[:, cols] = h_in[:, cols] * dec_exp[:, cols] + _dot_tn(bg, xw[:, cols])

    @pl.when(step == nc - 1)
    def _():
        hout_ref[0] = h_scr[...]

    if epilogue:
        y = y_scr[...] + yf_ref[...] + dskip_ref[...] * xs.astype(F32)
        y = y * _silu(z_ref[...])
        for g in range(G_B):
            cols = slice(g * R_B * HEAD_P, (g + 1) * R_B * HEAD_P)
            y_ref[:, cols] = _rms(y[:, cols], gn_ref[:, cols]).astype(y_ref.dtype)
    else:
        y_ref[...] = y_scr[...]


def _ssd(xbc, dt, dtt, consts, h0, *, nb, seq, direction, epi=None):
    nc = seq // BLK
    m = nb * seq
    alog_c, alog_r, xexp = consts[direction]

    def tok(b_, c_):
        return b_ * nc + (c_ if direction == 0 else nc - 1 - c_)

    const2 = lambda b_, c_: (0, 0)
    in_specs = [pl.BlockSpec((BLK, D_INNER), lambda b_, c_: (tok(b_, c_), 0)),
                pl.BlockSpec((BLK, G_B * N_STATE), lambda b_, c_: (tok(b_, c_), D_INNER // (G_B * N_STATE))),
                pl.BlockSpec((BLK, G_B * N_STATE), lambda b_, c_: (tok(b_, c_), D_INNER // (G_B * N_STATE) + 1)),
                pl.BlockSpec((BLK, DT_PAD), lambda b_, c_: (tok(b_, c_), 0)),
                pl.BlockSpec((2 * H_B, BLK), lambda b_, c_: (0, tok(b_, c_))),
                pl.BlockSpec((1, DT_PAD), const2),
                pl.BlockSpec((H_B, BLK), const2),
                pl.BlockSpec((3 * DT_PAD, D_INNER), const2),
                pl.BlockSpec((1, N_STATE, D_INNER), lambda b_, c_: (b_, 0, 0))]
    args = [xbc, xbc, xbc, dt, dtt, alog_c, alog_r, xexp, h0]
    if epi is not None:
        y_fwd, proj, d_skip, gn_g = epi
        in_specs += [pl.BlockSpec((BLK, D_INNER), lambda b_, c_: (tok(b_, c_), 0)),
                     pl.BlockSpec((BLK, D_INNER), lambda b_, c_: (tok(b_, c_), 2 * D_A // D_INNER)),
                     pl.BlockSpec((1, D_INNER), const2),
                     pl.BlockSpec((1, D_INNER), const2)]
        args += [y_fwd, proj, d_skip, gn_g]
    return pl.pallas_call(
        functools.partial(_ssd_body, direction=direction, nc=nc, epilogue=epi is not None),
        grid=(nb, nc),
        in_specs=in_specs,
        out_specs=[pl.BlockSpec((BLK, D_INNER), lambda b_, c_: (tok(b_, c_), 0)),
                   pl.BlockSpec((1, N_STATE, D_INNER), lambda b_, c_: (b_, 0, 0))],
        out_shape=[jax.ShapeDtypeStruct((m, D_INNER), BF16 if epi is not None else F32),
                   jax.ShapeDtypeStruct((nb, N_STATE, D_INNER), F32)],
        scratch_shapes=[pltpu.VMEM((N_STATE, D_INNER), F32), pltpu.VMEM((BLK, D_INNER), F32)],
        compiler_params=_params("parallel", "arbitrary"),
        name="ssd_bwd_gate" if epi is not None else "ssd_fwd",
    )(*args)


def _proj_resid_body(a1_ref, a2_ref, w_ref, h_ref, mod_ref, g_ref, o_ref, acc_ref, *, nk1, nk, gate_idx):
    k = pl.program_id(1)

    @pl.when(k == 0)
    def _():
        acc_ref[...] = _dot(a1_ref[...], w_ref[...])

    @pl.when((k > 0) & (k < nk1))
    def _():
        acc_ref[...] += _dot(a1_ref[...], w_ref[...])

    @pl.when(k >= nk1)
    def _():
        acc_ref[...] += _dot(a2_ref[...], w_ref[...])

    @pl.when(k == nk - 1)
    def _():
        _gated_norm_residual_rows(acc_ref, h_ref, g_ref, mod_ref, gate_idx, o_ref)


def _proj_resid(a1, a2, w, h, mods, g, *, gate_idx, tm, tk):
    m, k1 = a1.shape
    n = w.shape[1]
    tm = min(tm, m // mods.shape[0])
    tk = min(tk, k1)
    nk1 = k1 // tk
    nk = 2 * nk1
    rows_per_mod = m // mods.shape[0]
    assert rows_per_mod % tm == 0 and a2.shape == a1.shape and w.shape[0] == 2 * k1
    return pl.pallas_call(
        functools.partial(_proj_resid_body, nk1=nk1, nk=nk, gate_idx=gate_idx),
        grid=(m // tm, nk),
        in_specs=[pl.BlockSpec((tm, tk), lambda i, k: (i, jnp.minimum(k, nk1 - 1))),
                  pl.BlockSpec((tm, tk), lambda i, k: (i, jnp.maximum(k - nk1, 0))),
                  pl.BlockSpec((tk, n), lambda i, k: (k, 0)),
                  pl.BlockSpec((tm, n), lambda i, k: (i, 0)),
                  pl.BlockSpec((1, N_MOD, n), lambda i, k: ((i * tm) // rows_per_mod, 0, 0)),
                  pl.BlockSpec((1, n), lambda i, k: (0, 0))],
        out_specs=pl.BlockSpec((tm, n), lambda i, k: (i, 0)),
        out_shape=jax.ShapeDtypeStruct((m, n), F32),
        scratch_shapes=[pltpu.VMEM((tm, n), F32)],
        compiler_params=_params("parallel", "arbitrary"),
        name="proj_resid",
    )(a1, a2, w, h, mods, g.reshape(1, n))


def _mlp_body(h_ref, mod_ref, gpre_ref, gpost_ref, w1_ref, w2_ref, o_ref, a_ref, acc_ref, *, nf):
    f = pl.program_id(1)

    @pl.when(f == 0)
    def _():
        _norm_modulate_rows(h_ref, gpre_ref, mod_ref, a_ref, 3, 4)
        acc_ref[...] = jnp.zeros_like(acc_ref)

    hid = jnp.square(jnp.maximum(_dot(a_ref[...], w1_ref[...]), 0.0)).astype(BF16)
    acc_ref[...] += _dot(hid, w2_ref[...])

    @pl.when(f == nf - 1)
    def _():
        _gated_norm_residual_rows(acc_ref, h_ref, gpost_ref, mod_ref, 5, o_ref)


def _mlp(h, mods, g_pre, g_post, w1, w2, *, tm, tf):
    m, d = h.shape
    ff = w1.shape[1]
    tm = min(tm, m // mods.shape[0])
    rows_per_mod = m // mods.shape[0]
    assert rows_per_mod % tm == 0 and ff % tf == 0
    nf = ff // tf
    return pl.pallas_call(
        functools.partial(_mlp_body, nf=nf),
        grid=(m // tm, nf),
        in_specs=[pl.BlockSpec((tm, d), lambda i, f: (i, 0)),
                  pl.BlockSpec((1, N_MOD, d), lambda i, f: ((i * tm) // rows_per_mod, 0, 0)),
                  pl.BlockSpec((1, d), lambda i, f: (0, 0)),
                  pl.BlockSpec((1, d), lambda i, f: (0, 0)),
                  pl.BlockSpec((d, tf), lambda i, f: (0, f)),
                  pl.BlockSpec((tf, d), lambda i, f: (f, 0))],
        out_specs=pl.BlockSpec((tm, d), lambda i, f: (i, 0)),
        out_shape=jax.ShapeDtypeStruct((m, d), F32),
        scratch_shapes=[pltpu.VMEM((tm, d), BF16), pltpu.VMEM((tm, d), F32)],
        compiler_params=_params("parallel", "arbitrary"),
        name="mlp_sublayer",
    )(h, mods, g_pre.reshape(1, d), g_post.reshape(1, d), w1, w2)


_QC_HEADS = range(0, H_C)
_KC_HEADS = range(H_C, H_C + KV_C)
_QD_HEADS = range(H_C + 2 * KV_C, H_C + 2 * KV_C + H_D)
_KD_HEADS = range(H_C + 2 * KV_C + H_D, H_C + 2 * KV_C + H_D + KV_D)


def _qkv_prep_body(x_ref, cos_ref, sin_ref, qg_ref, kg_ref, o_ref, *, rope):
    lane = lax.broadcasted_iota(jnp.int32, (x_ref.shape[0], HEAD_DIM), 1)
    first_half = (lane % AXIS_DIM) < (AXIS_DIM // 2)
    for hd in range(CD_IN // HEAD_DIM):
        cols = slice(hd * HEAD_DIM, (hd + 1) * HEAD_DIM)
        x = x_ref[:, cols]
        if hd in _QC_HEADS:
            x = _rms(x, qg_ref[...])
        elif hd in _KC_HEADS:
            x = _rms(x, kg_ref[...])
        if rope and (hd in _QC_HEADS or hd in _KC_HEADS or hd in _QD_HEADS or hd in _KD_HEADS):
            partner = jnp.where(first_half,
                                pltpu.roll(x, HEAD_DIM - AXIS_DIM // 2, 1),
                                pltpu.roll(x, AXIS_DIM // 2, 1))
            x = x * cos_ref[...] + partner * sin_ref[...]
        o_ref[:, cols] = x.astype(o_ref.dtype)


def _qkv_prep(proj, cos, sin_signed, q_g, k_g, *, seq, rope, tr):
    m = proj.shape[0]
    tr = min(tr, seq)
    per_seq = seq // tr
    return pl.pallas_call(
        functools.partial(_qkv_prep_body, rope=rope),
        grid=(m // tr,),
        in_specs=[pl.BlockSpec((tr, CD_IN), lambda i: (i, 0)),
                  pl.BlockSpec((tr, HEAD_DIM), lambda i: (i % per_seq, 0)),
                  pl.BlockSpec((tr, HEAD_DIM), lambda i: (i % per_seq, 0)),
                  pl.BlockSpec((1, HEAD_DIM), lambda i: (0, 0)),
                  pl.BlockSpec((1, HEAD_DIM), lambda i: (0, 0))],
        out_specs=pl.BlockSpec((tr, CD_IN), lambda i: (i, 0)),
        out_shape=jax.ShapeDtypeStruct((m, CD_IN), BF16),
        compiler_params=_params("parallel"),
        name="qkv_prep",
    )(proj, cos, sin_signed, q_g.reshape(1, HEAD_DIM), k_g.reshape(1, HEAD_DIM))


def _attn_c_body(q_ref, k_ref, v_ref, kc_ref, vc_ref, o_ref):
    k = k_ref[...]
    v = v_ref[...]
    kc = kc_ref[...]
    vc = vc_ref[...]
    for r in range(H_C // KV_C):
        cols = slice(r * HEAD_DIM, (r + 1) * HEAD_DIM)
        q = q_ref[:, cols]
        s1 = _dot_nt(q, k)
        s2 = _dot_nt(q, kc)
        mx = jnp.maximum(jnp.max(s1, axis=-1, keepdims=True), jnp.max(s2, axis=-1, keepdims=True))
        p1 = jnp.exp2((s1 - mx) * (ATTN_SCALE * LOG2_E))
        p2 = jnp.exp2((s2 - mx) * (ATTN_SCALE * LOG2_E))
        den = jnp.sum(p1, axis=-1, keepdims=True) + jnp.sum(p2, axis=-1, keepdims=True)
        o = _dot(p1.astype(BF16), v) + _dot(p2.astype(BF16), vc)
        o_ref[:, cols] = (o / den).astype(o_ref.dtype)


def _attn_c(qkv, qkv_ctx, *, nb, seq, ctx_len, tq):
    tq = min(tq, seq)
    nq = seq // tq
    rq = H_C // KV_C
    k0 = H_C
    v0 = H_C + KV_C
    return pl.pallas_call(
        _attn_c_body,
        grid=(nb, KV_C, nq),
        in_specs=[pl.BlockSpec((tq, rq * HEAD_DIM), lambda b_, h, i: (b_ * nq + i, h)),
                  pl.BlockSpec((seq, HEAD_DIM), lambda b_, h, i: (b_, k0 + h)),
                  pl.BlockSpec((seq, HEAD_DIM), lambda b_, h, i: (b_, v0 + h)),
                  pl.BlockSpec((ctx_len, HEAD_DIM), lambda b_, h, i: (b_, k0 + h)),
                  pl.BlockSpec((ctx_len, HEAD_DIM), lambda b_, h, i: (b_, v0 + h))],
        out_specs=pl.BlockSpec((tq, rq * HEAD_DIM), lambda b_, h, i: (b_ * nq + i, h)),
        out_shape=jax.ShapeDtypeStruct((nb * seq, H_C * HEAD_DIM), BF16),
        compiler_params=_params("parallel", "parallel", "arbitrary"),
        name="attn_full",
    )(qkv, qkv, qkv, qkv_ctx, qkv_ctx)


def _attn_d_body(q_ref, kp_ref, kn_ref, kx_ref, vp_ref, vn_ref, vx_ref, kc_ref, vc_ref, sink_ref, o_ref, *, ntile):
    t = pl.program_id(2)
    rq = H_D // KV_D
    tq = q_ref.shape[0]
    nloc = tq + 2 * BLK
    q = jnp.concatenate([q_ref[:, r * HEAD_DIM:(r + 1) * HEAD_DIM] for r in range(rq)], axis=0)
    k_all = jnp.concatenate([kp_ref[...], kn_ref[...], kx_ref[...], kc_ref[...]], axis=0)
    v_all = jnp.concatenate([vp_ref[...], vn_ref[...], vx_ref[...], vc_ref[...]], axis=0)
    s = _dot_nt(q, k_all) * ATTN_SCALE

    ii = lax.broadcasted_iota(jnp.int32, (tq, nloc), 0)
    jj = lax.broadcasted_iota(jnp.int32, (tq, nloc), 1)
    d = jj - ii
    lo = jnp.where(t == 0, BLK, 0)
    hi = jnp.where(t == ntile - 1, nloc - BLK, nloc)
    neg = -jnp.inf
    bias = jnp.where(d >= 0, jnp.where(d <= 2 * BLK, jnp.where(jj >= lo, jnp.where(jj < hi, 0.0, neg), neg), neg),
                     neg)
    s_loc = s[:, :nloc] + jnp.concatenate([bias] * rq, axis=0)
    s_ctx = s[:, nloc:]
    sk = jnp.concatenate([jnp.broadcast_to(sink_ref[0, r:r + 1, 0:1], (tq, 1)) for r in range(rq)], axis=0)
    mx = jnp.maximum(jnp.maximum(jnp.max(s_loc, axis=-1, keepdims=True), jnp.max(s_ctx, axis=-1, keepdims=True)), sk)
    p_loc = jnp.exp(s_loc - mx)
    p_ctx = jnp.exp(s_ctx - mx)
    den = jnp.sum(p_loc, axis=-1, keepdims=True) + jnp.sum(p_ctx, axis=-1, keepdims=True) + jnp.exp(sk - mx)
    p = jnp.concatenate([p_loc.astype(BF16), p_ctx.astype(BF16)], axis=1)
    o = _dot(p, v_all) / den
    for r in range(rq):
        o_ref[:, r * HEAD_DIM:(r + 1) * HEAD_DIM] = o[r * tq:(r + 1) * tq].astype(o_ref.dtype)


def _attn_d(qkv, qkv_ctx, sink, *, nb, seq, ctx_len):
    nblk = seq // BLK
    tq = 2 * BLK
    ntile = seq // tq
    rq = H_D // KV_D
    q0 = (H_C + 2 * KV_C) // rq
    k0 = H_C + 2 * KV_C + H_D
    v0 = k0 + KV_D

    def edge(col0, shift):
        def index(b_, h, t):
            return (b_ * nblk + jnp.clip(2 * t + shift, 0, nblk - 1), col0 + h)
        return pl.BlockSpec((BLK, HEAD_DIM), index)

    def tile(col0):
        return pl.BlockSpec((tq, HEAD_DIM), lambda b_, h, t: (b_ * ntile + t, col0 + h))

    return pl.pallas_call(
        functools.partial(_attn_d_body, ntile=ntile),
        grid=(nb, KV_D, ntile),
        in_specs=[pl.BlockSpec((tq, rq * HEAD_DIM), lambda b_, h, t: (b_ * ntile + t, q0 + h)),
                  edge(k0, -1), tile(k0), edge(k0, 2),
                  edge(v0, -1), tile(v0), edge(v0, 2),
                  pl.BlockSpec((ctx_len, HEAD_DIM), lambda b_, h, t: (b_, k0 + h)),
                  pl.BlockSpec((ctx_len, HEAD_DIM), lambda b_, h, t: (b_, v0 + h)),
                  pl.BlockSpec((1, rq, BLK), lambda b_, h, t: (h, 0, 0))],
        out_specs=pl.BlockSpec((tq, rq * HEAD_DIM), lambda b_, h, t: (b_ * ntile + t, h)),
        out_shape=jax.ShapeDtypeStruct((nb * seq, H_D * HEAD_DIM), BF16),
        compiler_params=_params("parallel", "parallel", "arbitrary"),
        name="attn_window",
    )(qkv, qkv, qkv, qkv, qkv, qkv, qkv, qkv_ctx, qkv_ctx, sink)


def _rope_tables(seq):
    t = jnp.arange(seq)
    pos = jnp.stack([t // GRID_W, t % GRID_W], axis=-1).astype(F32)
    inv_freq = ROPE_BASE ** (-jnp.arange(0, AXIS_DIM, 2, dtype=F32) / AXIS_DIM)
    ang = pos[:, :, None] * inv_freq
    cos, sin = jnp.cos(ang), jnp.sin(ang)
    cos_t = jnp.concatenate([cos[:, 0], cos[:, 0], cos[:, 1], cos[:, 1]], axis=-1)
    sin_t = jnp.concatenate([-sin[:, 0], sin[:, 0], -sin[:, 1], sin[:, 1]], axis=-1)
    return cos_t, sin_t


def _pad_dt(v):
    return jnp.zeros((1, DT_PAD), F32).at[0, :2 * H_B].set(v.reshape(-1))


def _ssd_consts(a_log):
    out = []
    head_of_lane = jnp.arange(D_INNER) // HEAD_P
    for direction in range(2):
        rows = jnp.arange(DT_PAD)[:, None] - direction * H_B
        sel = (rows == head_of_lane[None, :]).astype(BF16)
        out.append((_pad_dt(a_log),
                    jnp.broadcast_to(a_log[direction][:, None], (H_B, BLK)),
                    jnp.concatenate([sel, sel, sel], axis=0)))
    return out


def _ab_layer(h_x, h_c, mods_x, mods_c, norm_g, w_in, w_s, b_s, ln_g, ln_b, conv_w, conv_b, a_log, dt_bias,
              d_skip, gn_g, w_out, w1, w2, *, nb, seq, ctx_len):
    w_main = w_in[:, :AB_MAIN].astype(BF16)
    w_dt = jnp.zeros((D_MODEL, DT_PAD), F32).at[:, :2 * H_B].set(w_in[:, AB_MAIN:]).astype(BF16)
    ws = w_s.astype(BF16)
    bias = jnp.repeat(b_s.T, DH_A, axis=1)
    consts = _ssd_consts(a_log)
    b_dt = _pad_dt(dt_bias)
    dskip = jnp.repeat(d_skip, HEAD_P).reshape(1, D_INNER)
    gn = gn_g.reshape(1, D_INNER)
    w_out_b = w_out.astype(BF16)
    w1_b = w1.astype(BF16)
    w2_b = w2.astype(BF16)

    def project(h, mods):
        proj, dt = _nmm(h, norm_g[0], mods, w_main, w_dt, b_dt, shift_idx=0, scale_idx=1, out_dtype=F32,
                        tm=1024, tn=1024)
        return proj, dt, dt[:, :2 * H_B].T

    def scans(proj, dt, dtt, n_tok, h0):
        xbc = _conv_silu(proj, conv_w, conv_b, nb=nb, seq=n_tok)
        y_f, hf = _ssd(xbc, dt, dtt, consts, h0[0], nb=nb, seq=n_tok, direction=0)
        y_b, hb = _ssd(xbc, dt, dtt, consts, h0[1], nb=nb, seq=n_tok, direction=1, epi=(y_f, proj, dskip, gn))
        return y_b, (hf, hb)

    def finish(h, mods, ya, yb):
        h = _proj_resid(ya, yb, w_out_b, h, mods, norm_g[1], gate_idx=2, tm=512, tk=1024)
        return _mlp(h, mods, norm_g[2], norm_g[3], w1_b, w2_b, tm=512, tf=512)

    proj_c, dt_c, dtt_c = project(h_c, mods_c)
    proj_x, dt_x, dtt_x = project(h_x, mods_x)
    zero = jnp.zeros((nb, N_STATE, D_INNER), F32)
    yb_c, st_c = scans(proj_c, dt_c, dtt_c, ctx_len, (zero, zero))
    yb_x, _ = scans(proj_x, dt_x, dtt_x, seq, st_c)
    ya_c = _gmlp(proj_c, ws, bias, ln_g, ln_b, tr=512)
    ya_x = _gmlp(proj_x, ws, bias, ln_g, ln_b, tr=512)
    return finish(h_x, mods_x, ya_x, yb_x), finish(h_c, mods_c, ya_c, yb_c)


def _cd_layer(h_x, h_c, mods_x, mods_c, norm_g, w_in, q_g, k_g, sink, w_out, w1, w2, *, nb, seq, ctx_len):
    w_in_b = w_in.astype(BF16)
    cos_t, sin_t = _rope_tables(seq)
    proj_x = _nmm(h_x, norm_g[0], mods_x, w_in_b, shift_idx=0, scale_idx=1, out_dtype=F32, tm=1024, tn=1024)
    proj_c = _nmm(h_c, norm_g[0], mods_c, w_in_b, shift_idx=0, scale_idx=1, out_dtype=F32, tm=1024, tn=1024)
    qkv_x = _qkv_prep(proj_x, cos_t, sin_t, q_g, k_g, seq=seq, rope=True, tr=512)
    qkv_c = _qkv_prep(proj_c, cos_t, sin_t, q_g, k_g, seq=ctx_len, rope=False, tr=256)
    oc = _attn_c(qkv_x, qkv_c, nb=nb, seq=seq, ctx_len=ctx_len, tq=512)
    sink_t = jnp.broadcast_to(sink.reshape(KV_D, H_D // KV_D, 1), (KV_D, H_D // KV_D, BLK))
    od = _attn_d(qkv_x, qkv_c, sink_t, nb=nb, seq=seq, ctx_len=ctx_len)
    h = _proj_resid(oc, od, w_out.astype(BF16), h_x, mods_x, norm_g[1], gate_idx=2, tm=512, tk=1024)
    return _mlp(h, mods_x, norm_g[2], norm_g[3], w1.astype(BF16), w2.astype(BF16), tm=512, tf=512)


def kernel(x, c, ctx, c_ctx, mod_w, mod_b, norm_g, mlp_w1, mlp_w2, ab_w_in, a_w_s, a_b_s, a_ln_g, a_ln_b,
           b_conv_w, b_conv_b, b_a_log, b_dt_bias, b_d, b_norm_g, ab_w_out, cd_w_in, c_q_norm_g, c_k_norm_g,
           d_sink, cd_w_out):
    nb, seq, d = x.shape
    ctx_len = ctx.shape[1]
    depth = mod_w.shape[0]
    assert depth == 2, "the odd layer is implemented as the last layer (no context update)"
    h_x = x.reshape(nb * seq, d)
    h_c = ctx.reshape(nb * ctx_len, d)
    cond_rows = -(-(nb + 1) // 8) * 8
    cond = jnp.zeros((cond_rows, d), F32).at[:nb].set(c).at[nb].set(c_ctx)
    for i in range(depth):
        j = i // 2
        mods = _modvec(cond, mod_w, mod_b, i)
        mods_x = mods[:nb].reshape(nb, N_MOD, d)
        mods_c = mods[nb:nb + 1].reshape(1, N_MOD, d)
        if i % 2 == 0:
            h_x, h_c = _ab_layer(h_x, h_c, mods_x, mods_c, norm_g[i], ab_w_in[j], a_w_s[j], a_b_s[j], a_ln_g[j],
                                 a_ln_b[j], b_conv_w[j], b_conv_b[j], b_a_log[j], b_dt_bias[j], b_d[j],
                                 b_norm_g[j], ab_w_out[j], mlp_w1[i], mlp_w2[i], nb=nb, seq=seq, ctx_len=ctx_len)
        else:
            h_x = _cd_layer(h_x, h_c, mods_x, mods_c, norm_g[i], cd_w_in[j], c_q_norm_g[j], c_k_norm_g[j],
                            d_sink[j], cd_w_out[j], mlp_w1[i], mlp_w2[i], nb=nb, seq=seq, ctx_len=ctx_len)
    return h_x.reshape(nb, seq, d)
```

```python
import functools
import math

import jax
import jax.numpy as jnp
from jax import lax
from jax.experimental import pallas as pl
from jax.experimental.pallas import tpu as pltpu

F32 = jnp.float32
BF16 = jnp.bfloat16

D_MODEL = 2048
GRID_W = 64
BLK = 128
EPS = 1e-6
N_MOD = 6

D_A = 2048
H_A = 8
DH_A = D_A // H_A

D_INNER = 2048
HEAD_P = 64
H_B = D_INNER // HEAD_P
G_B = 4
R_B = H_B // G_B
N_STATE = 128
CONV_W = 5
CONV_DIM = D_INNER + 2 * G_B * N_STATE

HEAD_DIM = 128
H_C = 8
KV_C = 2
H_D = 8
KV_D = 2
ROPE_BASE = 10000.0
AXIS_DIM = HEAD_DIM // 2
D_FF = 4 * D_MODEL
CD_IN = (H_C + 2 * KV_C + H_D + 2 * KV_D) * HEAD_DIM
ATTN_SCALE = HEAD_DIM ** -0.5
LOG2_E = math.log2(math.e)

AB_MAIN = 2 * D_A + D_INNER + CONV_DIM
DT_PAD = 128

VMEM_LIMIT_V7X = 56 * 1024 * 1024


def _params(*sem, vmem=VMEM_LIMIT_V7X):
    return pltpu.CompilerParams(dimension_semantics=sem, vmem_limit_bytes=vmem)


def _rms(x, g):
    return x * lax.rsqrt(jnp.mean(x * x, axis=-1, keepdims=True) + EPS) * g


def _silu(x):
    return x * jax.nn.sigmoid(x)


def _gelu(x):
    return 0.5 * x * (1.0 + lax.erf(x * (1.0 / math.sqrt(2.0))))


def _softplus(x):
    return jnp.maximum(x, 0.0) + jnp.log1p(jnp.exp(-jnp.abs(x)))


def _dot(a, b):
    return jnp.dot(a, b, preferred_element_type=F32)


def _dot_nt(a, b):
    return lax.dot_general(a, b, (((1,), (1,)), ((), ())), preferred_element_type=F32)


def _dot_tn(a, b):
    return lax.dot_general(a, b, (((0,), (0,)), ((), ())), preferred_element_type=F32)


def _norm_modulate_rows(x_ref, g_ref, mod_ref, a_ref, shift_idx, scale_idx):
    gs = g_ref[...] * (1.0 + mod_ref[0, scale_idx:scale_idx + 1, :])
    sh = mod_ref[0, shift_idx:shift_idx + 1, :]

    def strip(i, carry):
        r = pl.ds(pl.multiple_of(i * 16, 16), 16)
        x = x_ref[r, :]
        rs = lax.rsqrt(jnp.mean(x * x, axis=-1, keepdims=True) + EPS)
        a_ref[r, :] = (x * rs * gs + sh).astype(a_ref.dtype)
        return carry

    lax.fori_loop(0, x_ref.shape[0] // 16, strip, 0, unroll=8)


def _gated_norm_residual_rows(y_ref, h_ref, g_ref, mod_ref, gate_idx, o_ref):
    gg = mod_ref[0, gate_idx:gate_idx + 1, :] * g_ref[...]

    def strip(i, carry):
        r = pl.ds(pl.multiple_of(i * 8, 8), 8)
        y = y_ref[r, :]
        rs = lax.rsqrt(jnp.mean(y * y, axis=-1, keepdims=True) + EPS)
        o_ref[r, :] = h_ref[r, :] + y * rs * gg
        return carry

    lax.fori_loop(0, y_ref.shape[0] // 8, strip, 0, unroll=16)


def _modvec_body(s_ref, w_ref, b_ref, o_ref):
    a = _silu(s_ref[...]).astype(BF16)
    o_ref[...] = _dot(a, w_ref[...].astype(BF16)) + b_ref[...]


def _modvec(s, w, b, layer):
    r, d = s.shape
    depth, _, n = w.shape
    tn = 1024
    return pl.pallas_call(
        _modvec_body,
        grid=(n // tn,),
        in_specs=[pl.BlockSpec((r, d), lambda j: (0, 0)),
                  pl.BlockSpec((None, d, tn), lambda j: (layer, 0, j)),
                  pl.BlockSpec((None, 1, tn), lambda j: (layer, 0, j))],
        out_specs=pl.BlockSpec((r, tn), lambda j: (0, j)),
        out_shape=jax.ShapeDtypeStruct((r, n), F32),
        compiler_params=_params("arbitrary"),
        name="modvec",
    )(s, w, b.reshape(depth, 1, n))


def _nmm_body(*refs, shift_idx, scale_idx, side):
    if side:
        x_ref, g_ref, mod_ref, w_ref, ws_ref, bs_ref, o_ref, os_ref, a_ref = refs
    else:
        x_ref, g_ref, mod_ref, w_ref, o_ref, a_ref = refs

    @pl.when(pl.program_id(1) == 0)
    def _():
        _norm_modulate_rows(x_ref, g_ref, mod_ref, a_ref, shift_idx, scale_idx)
        if side:
            os_ref[...] = _softplus(_dot(a_ref[...], ws_ref[...]) + bs_ref[...])

    o_ref[...] = _dot(a_ref[...], w_ref[...]).astype(o_ref.dtype)


def _nmm(x, g, mods, w, w_side=None, b_side=None, *, shift_idx, scale_idx, out_dtype, tm, tn):
    m, k = x.shape
    n = w.shape[1]
    tm = min(tm, m // mods.shape[0])
    tn = min(tn, n)
    rows_per_mod = m // mods.shape[0]
    assert rows_per_mod % tm == 0 and n % tn == 0 and m % tm == 0
    side = w_side is not None
    in_specs = [pl.BlockSpec((tm, k), lambda i, j: (i, 0)),
                pl.BlockSpec((1, k), lambda i, j: (0, 0)),
                pl.BlockSpec((1, N_MOD, k), lambda i, j: ((i * tm) // rows_per_mod, 0, 0)),
                pl.BlockSpec((k, tn), lambda i, j: (0, j))]
    out_specs = [pl.BlockSpec((tm, tn), lambda i, j: (i, j))]
    out_shape = [jax.ShapeDtypeStruct((m, n), out_dtype)]
    args = [x, g.reshape(1, k), mods, w]
    if side:
        ns = w_side.shape[1]
        in_specs += [pl.BlockSpec((k, ns), lambda i, j: (0, 0)), pl.BlockSpec((1, ns), lambda i, j: (0, 0))]
        out_specs.append(pl.BlockSpec((tm, ns), lambda i, j: (i, 0)))
        out_shape.append(jax.ShapeDtypeStruct((m, ns), F32))
        args += [w_side, b_side]
    out = pl.pallas_call(
        functools.partial(_nmm_body, shift_idx=shift_idx, scale_idx=scale_idx, side=side),
        grid=(m // tm, n // tn),
        in_specs=in_specs,
        out_specs=out_specs,
        out_shape=out_shape,
        scratch_shapes=[pltpu.VMEM((tm, k), BF16)],
        compiler_params=_params("parallel", "arbitrary"),
        name="norm_mod_matmul",
    )(*args)
    return out if side else out[0]


def _gmlp_body(u_ref, v_ref, lng_ref, lnb_ref, ws_ref, bias_ref, o_ref, *, nchunk):
    v = _gelu(v_ref[...].astype(F32))
    mu = jnp.mean(v, axis=-1, keepdims=True)
    vc = v - mu
    vn = vc * lax.rsqrt(jnp.mean(vc * vc, axis=-1, keepdims=True) + EPS) * lng_ref[...] + lnb_ref[...]
    vb = vn.astype(BF16)
    for c in range(nchunk):
        rows = slice(c * BLK, (c + 1) * BLK)
        for h in range(H_A):
            cols = slice(h * DH_A, (h + 1) * DH_A)
            mix = _dot(ws_ref[h], vb[rows, cols]) + bias_ref[:, cols]
            u = _gelu(u_ref[rows, cols].astype(F32))
            o_ref[rows, cols] = (u * mix).astype(o_ref.dtype)


def _gmlp(proj, ws, bias, ln_g, ln_b, *, tr):
    m = proj.shape[0]
    tr = min(tr, m)
    return pl.pallas_call(
        functools.partial(_gmlp_body, nchunk=tr // BLK),
        grid=(m // tr,),
        in_specs=[pl.BlockSpec((tr, D_A), lambda i: (i, 0)),
                  pl.BlockSpec((tr, D_A), lambda i: (i, 1)),
                  pl.BlockSpec((1, D_A), lambda i: (0, 0)),
                  pl.BlockSpec((1, D_A), lambda i: (0, 0)),
                  pl.BlockSpec((H_A, BLK, BLK), lambda i: (0, 0, 0)),
                  pl.BlockSpec((BLK, D_A), lambda i: (0, 0))],
        out_specs=pl.BlockSpec((tr, D_A), lambda i: (i, 0)),
        out_shape=jax.ShapeDtypeStruct((m, D_A), BF16),
        compiler_params=_params("parallel"),
        name="gmlp_gate",
    )(proj, proj, ln_g.reshape(1, D_A), ln_b.reshape(1, D_A), ws, bias)


def _conv_body(x_ref, w_ref, b_ref, o_ref, *, seq):
    x = x_ref[...].astype(F32)
    row = lax.broadcasted_iota(jnp.int32, x.shape, 0)
    acc = x * w_ref[CONV_W // 2:CONV_W // 2 + 1, :] + b_ref[...]
    for k in range(CONV_W):
        d = k - CONV_W // 2
        if d == 0:
            continue
        shifted = pltpu.roll(x, (-d) % seq, 0)
        valid = (row >= -d) if d < 0 else (row < seq - d)
        acc = acc + jnp.where(valid, shifted, 0.0) * w_ref[k:k + 1, :]
    o_ref[...] = _silu(acc).astype(o_ref.dtype)


def _conv_silu(proj, w, b, *, nb, seq):
    tc = 512
    col0 = (AB_MAIN - CONV_DIM) // tc
    return pl.pallas_call(
        functools.partial(_conv_body, seq=seq),
        grid=(nb, CONV_DIM // tc),
        in_specs=[pl.BlockSpec((seq, tc), lambda b_, j: (b_, col0 + j)),
                  pl.BlockSpec((CONV_W, tc), lambda b_, j: (0, j)),
                  pl.BlockSpec((1, tc), lambda b_, j: (0, j))],
        out_specs=pl.BlockSpec((seq, tc), lambda b_, j: (b_, j)),
        out_shape=jax.ShapeDtypeStruct((nb * seq, CONV_DIM), BF16),
        compiler_params=_params("parallel", "parallel"),
        name="conv_silu",
    )(proj, w, b.reshape(1, CONV_DIM))


def _split3(v):
    hi = v.astype(BF16)
    r1 = v - hi.astype(F32)
    mid = r1.astype(BF16)
    lo = (r1 - mid.astype(F32)).astype(BF16)
    return jnp.concatenate([hi, mid, lo], axis=1)


def _ssd_body(*refs, direction, nc, epilogue):
    if epilogue:
        (xs_ref, b_ref, c_ref, dt_ref, dtt_ref, alogc_ref, alogr_ref, xexp_ref, h0_ref,
         yf_ref, z_ref, dskip_ref, gn_ref, y_ref, hout_ref, h_scr, y_scr) = refs
    else:
        (xs_ref, b_ref, c_ref, dt_ref, dtt_ref, alogc_ref, alogr_ref, xexp_ref, h0_ref,
         y_ref, hout_ref, h_scr, y_scr) = refs
    step = pl.program_id(1)

    @pl.when(step == 0)
    def _():
        h_scr[...] = h0_ref[0]

    off = direction * H_B
    dt_c = dt_ref[...]
    dta_c = dt_c * (-jnp.exp(alogc_ref[...]))
    dt_r = dtt_ref[off:off + H_B, :]
    dta_r = dt_r * (-jnp.exp(alogr_ref[...]))

    ii = lax.broadcasted_iota(jnp.int32, (BLK, BLK), 0)
    jj = lax.broadcasted_iota(jnp.int32, (BLK, BLK), 1)
    tri = (jj <= ii) if direction == 0 else (jj >= ii)
    tri_f = tri.astype(F32)
    s_c = jnp.dot(tri_f, dta_c, precision=lax.Precision.HIGHEST, preferred_element_type=F32)
    s_r = lax.dot_general(dta_r, tri_f, (((1,), (1,)), ((), ())), precision=lax.Precision.HIGHEST,
                          preferred_element_type=F32)
    last = BLK - 1 if direction == 0 else 0
    tot_c = s_c[last:last + 1, :]

    lane_c = lax.broadcasted_iota(jnp.int32, (BLK, DT_PAD), 1)
    mine = (lane_c >= off) & (lane_c < off + H_B)
    w_c = jnp.where(mine, jnp.exp(tot_c - s_c) * dt_c, 0.0)
    w_exp = _dot(_split3(w_c), xexp_ref[...])
    lane_8 = lax.broadcasted_iota(jnp.int32, (8, DT_PAD), 1)
    dec_c = jnp.where((lane_8 >= off) & (lane_8 < off + H_B), jnp.broadcast_to(jnp.exp(tot_c), (8, DT_PAD)), 0.0)
    dec_exp = _dot(_split3(dec_c), xexp_ref[...])[0:1, :]

    xs = xs_ref[...]
    h_in = h_scr[...]
    h_in_b = h_in.astype(BF16)
    lane = lax.broadcasted_iota(jnp.int32, (BLK, BLK), 1)
    first_head = lane < HEAD_P

    for g in range(G_B):
        cg = c_ref[:, g * N_STATE:(g + 1) * N_STATE]
        bg = b_ref[:, g * N_STATE:(g + 1) * N_STATE]
        cb = _dot_nt(cg, bg)
        cg_f = cg.astype(F32)
        for pair in range(R_B // 2):
            cols = slice((g * (R_B // 2) + pair) * 2 * HEAD_P, (g * (R_B // 2) + pair + 1) * 2 * HEAD_P)
            rhs = jnp.concatenate([xs[:, cols], h_in_b[:, cols]], axis=0)
            outs = []
            for q in range(2):
                r = g * R_B + pair * 2 + q
                s_col = jnp.broadcast_to(s_c[:, off + r:off + r + 1], (BLK, BLK))
                seg = s_col - s_r[r:r + 1, :]
                decay = jnp.exp(jnp.where(tri, seg, -jnp.inf))
                m_in = (cb * decay * dt_r[r:r + 1, :]).astype(BF16)
                m_st = (cg_f * jnp.exp(s_col)).astype(BF16)
                outs.append(_dot(jnp.concatenate([m_in, m_st], axis=1), rhs))
            y_scr[:, cols] = jnp.where(first_head, outs[0], outs[1])

    xw = (xs.astype(F32) * w_exp).astype(BF16)
    for g in range(G_B):
        cols = slice(g * R_B * HEAD_P, (g + 1) * R_B * HEAD_P)
        bg = b_ref[:, g * N_STATE:(g + 1) * N_STATE]
        h_scr[:, cols] = h_in[:, cols] * dec_exp[:, cols] + _dot_tn(bg, xw[:, cols])

    @pl.when(step == nc - 1)
    def _():
        hout_ref[0] = h_scr[...]

    if epilogue:
        y = y_scr[...] + yf_ref[...] + dskip_ref[...] * xs.astype(F32)
        y = y * _silu(z_ref[...].astype(F32))
        for g in range(G_B):
            cols = slice(g * R_B * HEAD_P, (g + 1) * R_B * HEAD_P)
            y_ref[:, cols] = _rms(y[:, cols], gn_ref[:, cols]).astype(y_ref.dtype)
    else:
        y_ref[...] = y_scr[...]


def _ssd(xbc, dt, dtt, consts, h0, *, nb, seq, direction, epi=None):
    nc = seq // BLK
    m = nb * seq
    alog_c, alog_r, xexp = consts[direction]

    def tok(b_, c_):
        return b_ * nc + (c_ if direction == 0 else nc - 1 - c_)

    const2 = lambda b_, c_: (0, 0)
    in_specs = [pl.BlockSpec((BLK, D_INNER), lambda b_, c_: (tok(b_, c_), 0)),
                pl.BlockSpec((BLK, G_B * N_STATE), lambda b_, c_: (tok(b_, c_), D_INNER // (G_B * N_STATE))),
                pl.BlockSpec((BLK, G_B * N_STATE), lambda b_, c_: (tok(b_, c_), D_INNER // (G_B * N_STATE) + 1)),
                pl.BlockSpec((BLK, DT_PAD), lambda b_, c_: (tok(b_, c_), 0)),
                pl.BlockSpec((2 * H_B, BLK), lambda b_, c_: (0, tok(b_, c_))),
                pl.BlockSpec((1, DT_PAD), const2),
                pl.BlockSpec((H_B, BLK), const2),
                pl.BlockSpec((3 * DT_PAD, D_INNER), const2),
                pl.BlockSpec((1, N_STATE, D_INNER), lambda b_, c_: (b_, 0, 0))]
    args = [xbc, xbc, xbc, dt, dtt, alog_c, alog_r, xexp, h0]
    if epi is not None:
        y_fwd, proj, d_skip, gn_g = epi
        in_specs += [pl.BlockSpec((BLK, D_INNER), lambda b_, c_: (tok(b_, c_), 0)),
                     pl.BlockSpec((BLK, D_INNER), lambda b_, c_: (tok(b_, c_), 2 * D_A // D_INNER)),
                     pl.BlockSpec((1, D_INNER), const2),
                     pl.BlockSpec((1, D_INNER), const2)]
        args += [y_fwd, proj, d_skip, gn_g]
    return pl.pallas_call(
        functools.partial(_ssd_body, direction=direction, nc=nc, epilogue=epi is not None),
        grid=(nb, nc),
        in_specs=in_specs,
        out_specs=[pl.BlockSpec((BLK, D_INNER), lambda b_, c_: (tok(b_, c_), 0)),
                   pl.BlockSpec((1, N_STATE, D_INNER), lambda b_, c_: (b_, 0, 0))],
        out_shape=[jax.ShapeDtypeStruct((m, D_INNER), BF16 if epi is not None else F32),
                   jax.ShapeDtypeStruct((nb, N_STATE, D_INNER), F32)],
        scratch_shapes=[pltpu.VMEM((N_STATE, D_INNER), F32), pltpu.VMEM((BLK, D_INNER), F32)],
        compiler_params=_params("parallel", "arbitrary"),
        name="ssd_bwd_gate" if epi is not None else "ssd_fwd",
    )(*args)


def _proj_resid_body(a1_ref, a2_ref, w_ref, h_ref, mod_ref, g_ref, o_ref, y_ref, *, k1, gate_idx):
    y_ref[...] = _dot(a1_ref[...], w_ref[:k1, :]) + _dot(a2_ref[...], w_ref[k1:, :])
    _gated_norm_residual_rows(y_ref, h_ref, g_ref, mod_ref, gate_idx, o_ref)


def _proj_resid(a1, a2, w, h, mods, g, *, gate_idx, tm):
    m, k1 = a1.shape
    n = w.shape[1]
    tm = min(tm, m // mods.shape[0])
    rows_per_mod = m // mods.shape[0]
    assert rows_per_mod % tm == 0 and a2.shape == a1.shape and w.shape[0] == 2 * k1
    return pl.pallas_call(
        functools.partial(_proj_resid_body, k1=k1, gate_idx=gate_idx),
        grid=(m // tm,),
        in_specs=[pl.BlockSpec((tm, k1), lambda i: (i, 0)),
                  pl.BlockSpec((tm, k1), lambda i: (i, 0)),
                  pl.BlockSpec((2 * k1, n), lambda i: (0, 0), pipeline_mode=pl.Buffered(1)),
                  pl.BlockSpec((tm, n), lambda i: (i, 0)),
                  pl.BlockSpec((1, N_MOD, n), lambda i: ((i * tm) // rows_per_mod, 0, 0)),
                  pl.BlockSpec((1, n), lambda i: (0, 0))],
        out_specs=pl.BlockSpec((tm, n), lambda i: (i, 0)),
        out_shape=jax.ShapeDtypeStruct((m, n), F32),
        scratch_shapes=[pltpu.VMEM((tm, n), F32)],
        compiler_params=_params("parallel"),
        name="proj_resid",
    )(a1, a2, w, h, mods, g.reshape(1, n))


def _mlp_body(h_ref, mod_ref, gpre_ref, gpost_ref, w1_ref, w2_ref, o_ref, a_ref, acc_ref, *, nf):
    f = pl.program_id(1)

    @pl.when(f == 0)
    def _():
        _norm_modulate_rows(h_ref, gpre_ref, mod_ref, a_ref, 3, 4)

    hid = jnp.square(jnp.maximum(_dot(a_ref[...], w1_ref[...]), 0.0)).astype(BF16)

    @pl.when(f == 0)
    def _():
        acc_ref[...] = _dot(hid, w2_ref[...])

    @pl.when(f > 0)
    def _():
        acc_ref[...] += _dot(hid, w2_ref[...])

    @pl.when(f == nf - 1)
    def _():
        _gated_norm_residual_rows(acc_ref, h_ref, gpost_ref, mod_ref, 5, o_ref)


def _mlp(h, mods, g_pre, g_post, w1, w2, *, tm, tf):
    m, d = h.shape
    ff = w1.shape[1]
    tm = min(tm, m // mods.shape[0])
    rows_per_mod = m // mods.shape[0]
    assert rows_per_mod % tm == 0 and ff % tf == 0
    nf = ff // tf
    return pl.pallas_call(
        functools.partial(_mlp_body, nf=nf),
        grid=(m // tm, nf),
        in_specs=[pl.BlockSpec((tm, d), lambda i, f: (i, 0)),
                  pl.BlockSpec((1, N_MOD, d), lambda i, f: ((i * tm) // rows_per_mod, 0, 0)),
                  pl.BlockSpec((1, d), lambda i, f: (0, 0)),
                  pl.BlockSpec((1, d), lambda i, f: (0, 0)),
                  pl.BlockSpec((d, tf), lambda i, f: (0, f)),
                  pl.BlockSpec((tf, d), lambda i, f: (f, 0))],
        out_specs=pl.BlockSpec((tm, d), lambda i, f: (i, 0)),
        out_shape=jax.ShapeDtypeStruct((m, d), F32),
        scratch_shapes=[pltpu.VMEM((tm, d), BF16), pltpu.VMEM((tm, d), F32)],
        compiler_params=_params("parallel", "arbitrary"),
        name="mlp_sublayer",
    )(h, mods, g_pre.reshape(1, d), g_post.reshape(1, d), w1, w2)


_QC_HEADS = range(0, H_C)
_KC_HEADS = range(H_C, H_C + KV_C)
_QD_HEADS = range(H_C + 2 * KV_C, H_C + 2 * KV_C + H_D)
_KD_HEADS = range(H_C + 2 * KV_C + H_D, H_C + 2 * KV_C + H_D + KV_D)


def _qkv_prep_body(x_ref, cos_ref, sin_ref, qg_ref, kg_ref, o_ref, *, rope):
    lane = lax.broadcasted_iota(jnp.int32, (x_ref.shape[0], HEAD_DIM), 1)
    first_half = (lane % AXIS_DIM) < (AXIS_DIM // 2)
    for hd in range(CD_IN // HEAD_DIM):
        cols = slice(hd * HEAD_DIM, (hd + 1) * HEAD_DIM)
        x = x_ref[:, cols].astype(F32)
        if hd in _QC_HEADS:
            x = _rms(x, qg_ref[...])
        elif hd in _KC_HEADS:
            x = _rms(x, kg_ref[...])
        if rope and (hd in _QC_HEADS or hd in _KC_HEADS or hd in _QD_HEADS or hd in _KD_HEADS):
            partner = jnp.where(first_half,
                                pltpu.roll(x, HEAD_DIM - AXIS_DIM // 2, 1),
                                pltpu.roll(x, AXIS_DIM // 2, 1))
            x = x * cos_ref[...] + partner * sin_ref[...]
        o_ref[:, cols] = x.astype(o_ref.dtype)


def _qkv_prep(proj, cos, sin_signed, q_g, k_g, *, seq, rope, tr):
    m = proj.shape[0]
    tr = min(tr, seq)
    per_seq = seq // tr
    return pl.pallas_call(
        functools.partial(_qkv_prep_body, rope=rope),
        grid=(m // tr,),
        in_specs=[pl.BlockSpec((tr, CD_IN), lambda i: (i, 0)),
                  pl.BlockSpec((tr, HEAD_DIM), lambda i: (i % per_seq, 0)),
                  pl.BlockSpec((tr, HEAD_DIM), lambda i: (i % per_seq, 0)),
                  pl.BlockSpec((1, HEAD_DIM), lambda i: (0, 0)),
                  pl.BlockSpec((1, HEAD_DIM), lambda i: (0, 0))],
        out_specs=pl.BlockSpec((tr, CD_IN), lambda i: (i, 0)),
        out_shape=jax.ShapeDtypeStruct((m, CD_IN), BF16),
        compiler_params=_params("parallel"),
        name="qkv_prep",
    )(proj, cos, sin_signed, q_g.reshape(1, HEAD_DIM), k_g.reshape(1, HEAD_DIM))


def _attn_c_body(q_ref, k_ref, v_ref, kc_ref, vc_ref, o_ref):
    k = k_ref[...]
    v = v_ref[...]
    kc = kc_ref[...]
    vc = vc_ref[...]
    for r in range(H_C // KV_C):
        cols = slice(r * HEAD_DIM, (r + 1) * HEAD_DIM)
        q = q_ref[:, cols]
        s1 = _dot_nt(q, k)
        s2 = _dot_nt(q, kc)
        mx = jnp.maximum(jnp.max(s1, axis=-1, keepdims=True), jnp.max(s2, axis=-1, keepdims=True))
        p1 = jnp.exp2((s1 - mx) * (ATTN_SCALE * LOG2_E))
        p2 = jnp.exp2((s2 - mx) * (ATTN_SCALE * LOG2_E))
        den = jnp.sum(p1, axis=-1, keepdims=True) + jnp.sum(p2, axis=-1, keepdims=True)
        o = _dot(p1.astype(BF16), v) + _dot(p2.astype(BF16), vc)
        o_ref[:, cols] = (o / den).astype(o_ref.dtype)


def _attn_c(qkv, qkv_ctx, *, nb, seq, ctx_len, tq):
    tq = min(tq, seq)
    nq = seq // tq
    rq = H_C // KV_C
    k0 = H_C
    v0 = H_C + KV_C
    return pl.pallas_call(
        _attn_c_body,
        grid=(nb, KV_C, nq),
        in_specs=[pl.BlockSpec((tq, rq * HEAD_DIM), lambda b_, h, i: (b_ * nq + i, h)),
                  pl.BlockSpec((seq, HEAD_DIM), lambda b_, h, i: (b_, k0 + h)),
                  pl.BlockSpec((seq, HEAD_DIM), lambda b_, h, i: (b_, v0 + h)),
                  pl.BlockSpec((ctx_len, HEAD_DIM), lambda b_, h, i: (b_, k0 + h)),
                  pl.BlockSpec((ctx_len, HEAD_DIM), lambda b_, h, i: (b_, v0 + h))],
        out_specs=pl.BlockSpec((tq, rq * HEAD_DIM), lambda b_, h, i: (b_ * nq + i, h)),
        out_shape=jax.ShapeDtypeStruct((nb * seq, H_C * HEAD_DIM), BF16),
        compiler_params=_params("parallel", "parallel", "arbitrary"),
        name="attn_full",
    )(qkv, qkv, qkv, qkv_ctx, qkv_ctx)


def _attn_d_body(q_ref, kp_ref, kn_ref, kx_ref, vp_ref, vn_ref, vx_ref, kc_ref, vc_ref, sink_ref, o_ref, *, ntile):
    t = pl.program_id(2)
    rq = H_D // KV_D
    tq = q_ref.shape[0]
    nloc = tq + 2 * BLK
    q = jnp.concatenate([q_ref[:, r * HEAD_DIM:(r + 1) * HEAD_DIM] for r in range(rq)], axis=0)
    k_all = jnp.concatenate([kp_ref[...], kn_ref[...], kx_ref[...], kc_ref[...]], axis=0)
    v_all = jnp.concatenate([vp_ref[...], vn_ref[...], vx_ref[...], vc_ref[...]], axis=0)
    s = _dot_nt(q, k_all) * ATTN_SCALE

    ii = lax.broadcasted_iota(jnp.int32, (tq, nloc), 0)
    jj = lax.broadcasted_iota(jnp.int32, (tq, nloc), 1)
    d = jj - ii
    lo = jnp.where(t == 0, BLK, 0)
    hi = jnp.where(t == ntile - 1, nloc - BLK, nloc)
    neg = -jnp.inf
    bias = jnp.where(d >= 0, jnp.where(d <= 2 * BLK, jnp.where(jj >= lo, jnp.where(jj < hi, 0.0, neg), neg), neg),
                     neg)
    s_loc = s[:, :nloc] + jnp.concatenate([bias] * rq, axis=0)
    s_ctx = s[:, nloc:]
    sk = jnp.concatenate([jnp.broadcast_to(sink_ref[0, r:r + 1, 0:1], (tq, 1)) for r in range(rq)], axis=0)
    mx = jnp.maximum(jnp.maximum(jnp.max(s_loc, axis=-1, keepdims=True), jnp.max(s_ctx, axis=-1, keepdims=True)), sk)
    p_loc = jnp.exp(s_loc - mx)
    p_ctx = jnp.exp(s_ctx - mx)
    den = jnp.sum(p_loc, axis=-1, keepdims=True) + jnp.sum(p_ctx, axis=-1, keepdims=True) + jnp.exp(sk - mx)
    p = jnp.concatenate([p_loc.astype(BF16), p_ctx.astype(BF16)], axis=1)
    o = _dot(p, v_all) / den
    for r in range(rq):
        o_ref[:, r * HEAD_DIM:(r + 1) * HEAD_DIM] = o[r * tq:(r + 1) * tq].astype(o_ref.dtype)


def _attn_d(qkv, qkv_ctx, sink, *, nb, seq, ctx_len):
    nblk = seq // BLK
    tq = 2 * BLK
    ntile = seq // tq
    rq = H_D // KV_D
    q0 = (H_C + 2 * KV_C) // rq
    k0 = H_C + 2 * KV_C + H_D
    v0 = k0 + KV_D

    def edge(col0, shift):
        def index(b_, h, t):
            return (b_ * nblk + jnp.clip(2 * t + shift, 0, nblk - 1), col0 + h)
        return pl.BlockSpec((BLK, HEAD_DIM), index)

    def tile(col0):
        return pl.BlockSpec((tq, HEAD_DIM), lambda b_, h, t: (b_ * ntile + t, col0 + h))

    return pl.pallas_call(
        functools.partial(_attn_d_body, ntile=ntile),
        grid=(nb, KV_D, ntile),
        in_specs=[pl.BlockSpec((tq, rq * HEAD_DIM), lambda b_, h, t: (b_ * ntile + t, q0 + h)),
                  edge(k0, -1), tile(k0), edge(k0, 2),
                  edge(v0, -1), tile(v0), edge(v0, 2),
                  pl.BlockSpec((ctx_len, HEAD_DIM), lambda b_, h, t: (b_, k0 + h)),
                  pl.BlockSpec((ctx_len, HEAD_DIM), lambda b_, h, t: (b_, v0 + h)),
                  pl.BlockSpec((1, rq, BLK), lambda b_, h, t: (h, 0, 0))],
        out_specs=pl.BlockSpec((tq, rq * HEAD_DIM), lambda b_, h, t: (b_ * ntile + t, h)),
        out_shape=jax.ShapeDtypeStruct((nb * seq, H_D * HEAD_DIM), BF16),
        compiler_params=_params("parallel", "parallel", "arbitrary"),
        name="attn_window",
    )(qkv, qkv, qkv, qkv, qkv, qkv, qkv, qkv_ctx, qkv_ctx, sink)


def _rope_tables(seq):
    t = jnp.arange(seq)
    pos = jnp.stack([t // GRID_W, t % GRID_W], axis=-1).astype(F32)
    inv_freq = ROPE_BASE ** (-jnp.arange(0, AXIS_DIM, 2, dtype=F32) / AXIS_DIM)
    ang = pos[:, :, None] * inv_freq
    cos, sin = jnp.cos(ang), jnp.sin(ang)
    cos_t = jnp.concatenate([cos[:, 0], cos[:, 0], cos[:, 1], cos[:, 1]], axis=-1)
    sin_t = jnp.concatenate([-sin[:, 0], sin[:, 0], -sin[:, 1], sin[:, 1]], axis=-1)
    return cos_t, sin_t


def _pad_dt(v):
    return jnp.zeros((1, DT_PAD), F32).at[0, :2 * H_B].set(v.reshape(-1))


def _ssd_consts(a_log):
    out = []
    head_of_lane = jnp.arange(D_INNER) // HEAD_P
    for direction in range(2):
        rows = jnp.arange(DT_PAD)[:, None] - direction * H_B
        sel = (rows == head_of_lane[None, :]).astype(BF16)
        out.append((_pad_dt(a_log),
                    jnp.broadcast_to(a_log[direction][:, None], (H_B, BLK)),
                    jnp.concatenate([sel, sel, sel], axis=0)))
    return out


def _ab_layer(h_x, h_c, mods_x, mods_c, norm_g, w_in, w_s, b_s, ln_g, ln_b, conv_w, conv_b, a_log, dt_bias,
              d_skip, gn_g, w_out, w1, w2, *, nb, seq, ctx_len):
    w_main = w_in[:, :AB_MAIN].astype(BF16)
    w_dt = jnp.zeros((D_MODEL, DT_PAD), F32).at[:, :2 * H_B].set(w_in[:, AB_MAIN:]).astype(BF16)
    ws = w_s.astype(BF16)
    bias = jnp.repeat(b_s.T, DH_A, axis=1)
    consts = _ssd_consts(a_log)
    b_dt = _pad_dt(dt_bias)
    dskip = jnp.repeat(d_skip, HEAD_P).reshape(1, D_INNER)
    gn = gn_g.reshape(1, D_INNER)
    w_out_b = w_out.astype(BF16)
    w1_b = w1.astype(BF16)
    w2_b = w2.astype(BF16)

    def project(h, mods):
        proj, dt = _nmm(h, norm_g[0], mods, w_main, w_dt, b_dt, shift_idx=0, scale_idx=1, out_dtype=BF16,
                        tm=1024, tn=1024)
        return proj, dt, dt[:, :2 * H_B].T

    def scans(proj, dt, dtt, n_tok, h0):
        xbc = _conv_silu(proj, conv_w, conv_b, nb=nb, seq=n_tok)
        y_f, hf = _ssd(xbc, dt, dtt, consts, h0[0], nb=nb, seq=n_tok, direction=0)
        y_b, hb = _ssd(xbc, dt, dtt, consts, h0[1], nb=nb, seq=n_tok, direction=1, epi=(y_f, proj, dskip, gn))
        return y_b, (hf, hb)

    def finish(h, mods, ya, yb):
        h = _proj_resid(ya, yb, w_out_b, h, mods, norm_g[1], gate_idx=2, tm=512)
        return _mlp(h, mods, norm_g[2], norm_g[3], w1_b, w2_b, tm=512, tf=1024)

    proj_c, dt_c, dtt_c = project(h_c, mods_c)
    proj_x, dt_x, dtt_x = project(h_x, mods_x)
    zero = jnp.zeros((nb, N_STATE, D_INNER), F32)
    yb_c, st_c = scans(proj_c, dt_c, dtt_c, ctx_len, (zero, zero))
    yb_x, _ = scans(proj_x, dt_x, dtt_x, seq, st_c)
    ya_c = _gmlp(proj_c, ws, bias, ln_g, ln_b, tr=512)
    ya_x = _gmlp(proj_x, ws, bias, ln_g, ln_b, tr=512)
    return finish(h_x, mods_x, ya_x, yb_x), finish(h_c, mods_c, ya_c, yb_c)


def _cd_layer(h_x, h_c, mods_x, mods_c, norm_g, w_in, q_g, k_g, sink, w_out, w1, w2, *, nb, seq, ctx_len):
    w_in_b = w_in.astype(BF16)
    cos_t, sin_t = _rope_tables(seq)
    proj_x = _nmm(h_x, norm_g[0], mods_x, w_in_b, shift_idx=0, scale_idx=1, out_dtype=BF16, tm=1024, tn=1024)
    proj_c = _nmm(h_c, norm_g[0], mods_c, w_in_b, shift_idx=0, scale_idx=1, out_dtype=BF16, tm=1024, tn=1024)
    qkv_x = _qkv_prep(proj_x, cos_t, sin_t, q_g, k_g, seq=seq, rope=True, tr=512)
    qkv_c = _qkv_prep(proj_c, cos_t, sin_t, q_g, k_g, seq=ctx_len, rope=False, tr=256)
    oc = _attn_c(qkv_x, qkv_c, nb=nb, seq=seq, ctx_len=ctx_len, tq=512)
    sink_t = jnp.broadcast_to(sink.reshape(KV_D, H_D // KV_D, 1), (KV_D, H_D // KV_D, BLK))
    od = _attn_d(qkv_x, qkv_c, sink_t, nb=nb, seq=seq, ctx_len=ctx_len)
    h = _proj_resid(oc, od, w_out.astype(BF16), h_x, mods_x, norm_g[1], gate_idx=2, tm=512)
    return _mlp(h, mods_x, norm_g[2], norm_g[3], w1.astype(BF16), w2.astype(BF16), tm=512, tf=1024)


def kernel(x, c, ctx, c_ctx, mod_w, mod_b, norm_g, mlp_w1, mlp_w2, ab_w_in, a_w_s, a_b_s, a_ln_g, a_ln_b,
           b_conv_w, b_conv_b, b_a_log, b_dt_bias, b_d, b_norm_g, ab_w_out, cd_w_in, c_q_norm_g, c_k_norm_g,
           d_sink, cd_w_out):
    nb, seq, d = x.shape
    ctx_len = ctx.shape[1]
    depth = mod_w.shape[0]
    assert depth == 2, "the odd layer is implemented as the last layer (no context update)"
    h_x = x.reshape(nb * seq, d)
    h_c = ctx.reshape(nb * ctx_len, d)
    cond_rows = -(-(nb + 1) // 8) * 8
    cond = jnp.zeros((cond_rows, d), F32).at[:nb].set(c).at[nb].set(c_ctx)
    for i in range(depth):
        j = i // 2
        mods = _modvec(cond, mod_w, mod_b, i)
        mods_x = mods[:nb].reshape(nb, N_MOD, d)
        mods_c = mods[nb:nb + 1].reshape(1, N_MOD, d)
        if i % 2 == 0:
            h_x, h_c = _ab_layer(h_x, h_c, mods_x, mods_c, norm_g[i], ab_w_in[j], a_w_s[j], a_b_s[j], a_ln_g[j],
                                 a_ln_b[j], b_conv_w[j], b_conv_b[j], b_a_log[j], b_dt_bias[j], b_d[j],
                                 b_norm_g[j], ab_w_out[j], mlp_w1[i], mlp_w2[i], nb=nb, seq=seq, ctx_len=ctx_len)
        else:
            h_x = _cd_layer(h_x, h_c, mods_x, mods_c, norm_g[i], cd_w_in[j], c_q_norm_g[j], c_k_norm_g[j],
                            d_sink[j], cd_w_out[j], mlp_w1[i], mlp_w2[i], nb=nb, seq=seq, ctx_len=ctx_len)
    return h_x.reshape(nb, seq, d)
```

```python
import functools
import math

import jax
import jax.numpy as jnp
from jax import lax
from jax.experimental import pallas as pl
from jax.experimental.pallas import tpu as pltpu

F32 = jnp.float32
BF16 = jnp.bfloat16

D_MODEL = 2048
GRID_W = 64
BLK = 128
EPS = 1e-6
N_MOD = 6

D_A = 2048
H_A = 8
DH_A = D_A // H_A

D_INNER = 2048
HEAD_P = 64
H_B = D_INNER // HEAD_P
G_B = 4
R_B = H_B // G_B
N_STATE = 128
CONV_W = 5
CONV_DIM = D_INNER + 2 * G_B * N_STATE

HEAD_DIM = 128
H_C = 8
KV_C = 2
H_D = 8
KV_D = 2
ROPE_BASE = 10000.0
AXIS_DIM = HEAD_DIM // 2
D_FF = 4 * D_MODEL
CD_IN = (H_C + 2 * KV_C + H_D + 2 * KV_D) * HEAD_DIM
ATTN_SCALE = HEAD_DIM ** -0.5
LOG2_E = math.log2(math.e)

AB_MAIN = 2 * D_A + D_INNER + CONV_DIM
DT_PAD = 128

VMEM_LIMIT_V7X = 56 * 1024 * 1024


def _params(*sem, vmem=VMEM_LIMIT_V7X):
    return pltpu.CompilerParams(dimension_semantics=sem, vmem_limit_bytes=vmem)


def _rms(x, g):
    return x * lax.rsqrt(jnp.mean(x * x, axis=-1, keepdims=True) + EPS) * g


def _silu(x):
    return x * jax.nn.sigmoid(x)


def _gelu(x):
    return 0.5 * x * (1.0 + lax.erf(x * (1.0 / math.sqrt(2.0))))


def _softplus(x):
    return jnp.maximum(x, 0.0) + jnp.log1p(jnp.exp(-jnp.abs(x)))


def _dot(a, b):
    return jnp.dot(a, b, preferred_element_type=F32)


def _dot_nt(a, b):
    return lax.dot_general(a, b, (((1,), (1,)), ((), ())), preferred_element_type=F32)


def _dot_tn(a, b):
    return lax.dot_general(a, b, (((0,), (0,)), ((), ())), preferred_element_type=F32)


def _chunk_cumsum(x, reverse):
    n = x.shape[0]
    pos = lax.broadcasted_iota(jnp.int32, x.shape, 0) & (BLK - 1)
    k = 1
    while k < BLK:
        if reverse:
            x = x + jnp.where(pos < BLK - k, pltpu.roll(x, n - k, 0), 0.0)
        else:
            x = x + jnp.where(pos >= k, pltpu.roll(x, k, 0), 0.0)
        k *= 2
    return x


def _norm_modulate_rows(x_ref, g_ref, mod_ref, a_ref, shift_idx, scale_idx):
    gs = g_ref[...] * (1.0 + mod_ref[0, scale_idx:scale_idx + 1, :])
    sh = mod_ref[0, shift_idx:shift_idx + 1, :]

    def strip(i, carry):
        r = pl.ds(pl.multiple_of(i * 16, 16), 16)
        x = x_ref[r, :]
        rs = lax.rsqrt(jnp.mean(x * x, axis=-1, keepdims=True) + EPS)
        a_ref[r, :] = (x * rs * gs + sh).astype(a_ref.dtype)
        return carry

    lax.fori_loop(0, x_ref.shape[0] // 16, strip, 0, unroll=8)


def _gated_norm_residual_rows(y_ref, h_ref, g_ref, mod_ref, gate_idx, o_ref):
    gg = mod_ref[0, gate_idx:gate_idx + 1, :] * g_ref[...]

    def strip(i, carry):
        r = pl.ds(pl.multiple_of(i * 8, 8), 8)
        y = y_ref[r, :]
        rs = lax.rsqrt(jnp.mean(y * y, axis=-1, keepdims=True) + EPS)
        o_ref[r, :] = h_ref[r, :] + y * rs * gg
        return carry

    lax.fori_loop(0, y_ref.shape[0] // 8, strip, 0, unroll=16)


def _modvec_body(s_ref, w_ref, b_ref, o_ref):
    a = _silu(s_ref[...]).astype(BF16)
    o_ref[...] = _dot(a, w_ref[...].astype(BF16)) + b_ref[...]


def _modvec(s, w, b, layer):
    r, d = s.shape
    depth, _, n = w.shape
    tn = 1024
    return pl.pallas_call(
        _modvec_body,
        grid=(n // tn,),
        in_specs=[pl.BlockSpec((r, d), lambda j: (0, 0)),
                  pl.BlockSpec((None, d, tn), lambda j: (layer, 0, j)),
                  pl.BlockSpec((None, 1, tn), lambda j: (layer, 0, j))],
        out_specs=pl.BlockSpec((r, tn), lambda j: (0, j)),
        out_shape=jax.ShapeDtypeStruct((r, n), F32),
        compiler_params=_params("arbitrary"),
        name="modvec",
    )(s, w, b.reshape(depth, 1, n))


def _nmm_body(*refs, shift_idx, scale_idx, side, act_tiles):
    if side:
        x_ref, g_ref, mod_ref, w_ref, ws_ref, bs_ref, alog_ref, o_ref, dt_ref, cum_ref, a_ref = refs
    else:
        x_ref, g_ref, mod_ref, w_ref, o_ref, a_ref = refs

    @pl.when(pl.program_id(1) == 0)
    def _():
        _norm_modulate_rows(x_ref, g_ref, mod_ref, a_ref, shift_idx, scale_idx)
        if side:
            dt = _softplus(_dot(a_ref[...], ws_ref[...]) + bs_ref[...])
            dt_ref[...] = dt
            dta = dt * (-jnp.exp(alog_ref[...]))
            lane = lax.broadcasted_iota(jnp.int32, dta.shape, 1)
            cum_ref[...] = jnp.where(lane < H_B, _chunk_cumsum(dta, False), _chunk_cumsum(dta, True))

    if act_tiles is None:
        o_ref[...] = _dot(a_ref[...], w_ref[...]).astype(o_ref.dtype)
    else:
        j = pl.program_id(1)
        n_gelu, n_silu = act_tiles

        @pl.when(j < n_gelu)
        def _():
            o_ref[...] = _gelu(_dot(a_ref[...], w_ref[...])).astype(o_ref.dtype)

        @pl.when((j >= n_gelu) & (j < n_gelu + n_silu))
        def _():
            o_ref[...] = _silu(_dot(a_ref[...], w_ref[...])).astype(o_ref.dtype)

        @pl.when(j >= n_gelu + n_silu)
        def _():
            o_ref[...] = _dot(a_ref[...], w_ref[...]).astype(o_ref.dtype)


def _nmm(x, g, mods, w, w_side=None, b_side=None, alog_side=None, *, shift_idx, scale_idx, out_dtype, tm, tn,
         act_tiles=None):
    m, k = x.shape
    n = w.shape[1]
    tm = min(tm, m // mods.shape[0])
    tn = min(tn, n)
    rows_per_mod = m // mods.shape[0]
    assert rows_per_mod % tm == 0 and n % tn == 0 and m % tm == 0
    side = w_side is not None
    in_specs = [pl.BlockSpec((tm, k), lambda i, j: (i, 0)),
                pl.BlockSpec((1, k), lambda i, j: (0, 0)),
                pl.BlockSpec((1, N_MOD, k), lambda i, j: ((i * tm) // rows_per_mod, 0, 0)),
                pl.BlockSpec((k, tn), lambda i, j: (0, j))]
    out_specs = [pl.BlockSpec((tm, tn), lambda i, j: (i, j))]
    out_shape = [jax.ShapeDtypeStruct((m, n), out_dtype)]
    args = [x, g.reshape(1, k), mods, w]
    if side:
        ns = w_side.shape[1]
        in_specs += [pl.BlockSpec((k, ns), lambda i, j: (0, 0)), pl.BlockSpec((1, ns), lambda i, j: (0, 0)),
                     pl.BlockSpec((1, ns), lambda i, j: (0, 0))]
        out_specs += [pl.BlockSpec((tm, ns), lambda i, j: (i, 0))] * 2
        out_shape += [jax.ShapeDtypeStruct((m, ns), F32)] * 2
        args += [w_side, b_side, alog_side]
        assert tm % BLK == 0
    out = pl.pallas_call(
        functools.partial(_nmm_body, shift_idx=shift_idx, scale_idx=scale_idx, side=side, act_tiles=act_tiles),
        grid=(m // tm, n // tn),
        in_specs=in_specs,
        out_specs=out_specs,
        out_shape=out_shape,
        scratch_shapes=[pltpu.VMEM((tm, k), BF16)],
        compiler_params=_params("parallel", "arbitrary"),
        name="norm_mod_matmul",
    )(*args)
    return out if side else out[0]


def _gmlp_body(u_ref, v_ref, lng_ref, lnb_ref, ws_ref, bias_ref, o_ref, *, nchunk):
    v = v_ref[...].astype(F32)
    mu = jnp.mean(v, axis=-1, keepdims=True)
    vc = v - mu
    vn = vc * lax.rsqrt(jnp.mean(vc * vc, axis=-1, keepdims=True) + EPS) * lng_ref[...] + lnb_ref[...]
    vb = vn.astype(BF16)
    for c in range(nchunk):
        rows = slice(c * BLK, (c + 1) * BLK)
        for h in range(H_A):
            cols = slice(h * DH_A, (h + 1) * DH_A)
            mix = _dot(ws_ref[h], vb[rows, cols]) + bias_ref[:, cols]
            u = u_ref[rows, cols].astype(F32)
            o_ref[rows, cols] = (u * mix).astype(o_ref.dtype)


def _gmlp(proj, ws, bias, ln_g, ln_b, *, tr):
    m = proj.shape[0]
    tr = min(tr, m)
    return pl.pallas_call(
        functools.partial(_gmlp_body, nchunk=tr // BLK),
        grid=(m // tr,),
        in_specs=[pl.BlockSpec((tr, D_A), lambda i: (i, 0)),
                  pl.BlockSpec((tr, D_A), lambda i: (i, 1)),
                  pl.BlockSpec((1, D_A), lambda i: (0, 0)),
                  pl.BlockSpec((1, D_A), lambda i: (0, 0)),
                  pl.BlockSpec((H_A, BLK, BLK), lambda i: (0, 0, 0)),
                  pl.BlockSpec((BLK, D_A), lambda i: (0, 0))],
        out_specs=pl.BlockSpec((tr, D_A), lambda i: (i, 0)),
        out_shape=jax.ShapeDtypeStruct((m, D_A), BF16),
        compiler_params=_params("parallel"),
        name="gmlp_gate",
    )(proj, proj, ln_g.reshape(1, D_A), ln_b.reshape(1, D_A), ws, bias)


def _conv_body(x_ref, w_ref, b_ref, o_ref, *, seq):
    x = x_ref[...].astype(F32)
    row = lax.broadcasted_iota(jnp.int32, x.shape, 0)
    acc = x * w_ref[CONV_W // 2:CONV_W // 2 + 1, :] + b_ref[...]
    for k in range(CONV_W):
        d = k - CONV_W // 2
        if d == 0:
            continue
        shifted = pltpu.roll(x, (-d) % seq, 0)
        valid = (row >= -d) if d < 0 else (row < seq - d)
        acc = acc + jnp.where(valid, shifted, 0.0) * w_ref[k:k + 1, :]
    o_ref[...] = _silu(acc).astype(o_ref.dtype)


def _conv_silu(proj, w, b, *, nb, seq):
    tc = 512
    col0 = (AB_MAIN - CONV_DIM) // tc
    return pl.pallas_call(
        functools.partial(_conv_body, seq=seq),
        grid=(nb, CONV_DIM // tc),
        in_specs=[pl.BlockSpec((seq, tc), lambda b_, j: (b_, col0 + j)),
                  pl.BlockSpec((CONV_W, tc), lambda b_, j: (0, j)),
                  pl.BlockSpec((1, tc), lambda b_, j: (0, j))],
        out_specs=pl.BlockSpec((seq, tc), lambda b_, j: (b_, j)),
        out_shape=jax.ShapeDtypeStruct((nb * seq, CONV_DIM), BF16),
        compiler_params=_params("parallel", "parallel"),
        name="conv_silu",
    )(proj, w, b.reshape(1, CONV_DIM))


def _split3(v):
    hi = v.astype(BF16)
    r1 = v - hi.astype(F32)
    mid = r1.astype(BF16)
    lo = (r1 - mid.astype(F32)).astype(BF16)
    return jnp.concatenate([hi, mid, lo], axis=1)


def _ssd_body(*refs, direction, nc, epilogue):
    if epilogue:
        (xs_ref, b_ref, c_ref, dt_ref, cum_ref, dtt_ref, cumt_ref, xexp_ref, h0_ref,
         yf_ref, z_ref, dskip_ref, gn_ref, y_ref, hout_ref, h_scr, y_scr) = refs
    else:
        (xs_ref, b_ref, c_ref, dt_ref, cum_ref, dtt_ref, cumt_ref, xexp_ref, h0_ref,
         y_ref, hout_ref, h_scr, y_scr) = refs
    step = pl.program_id(1)

    @pl.when(step == 0)
    def _():
        h_scr[...] = h0_ref[0]

    off = direction * H_B
    dt_c = dt_ref[...]
    s_c = cum_ref[...]
    dt_r = dtt_ref[off:off + H_B, :]
    s_r = cumt_ref[off:off + H_B, :]

    ii = lax.broadcasted_iota(jnp.int32, (BLK, BLK), 0)
    jj = lax.broadcasted_iota(jnp.int32, (BLK, BLK), 1)
    tri = (jj <= ii) if direction == 0 else (jj >= ii)
    last = BLK - 1 if direction == 0 else 0
    tot_c = s_c[last:last + 1, :]

    lane_c = lax.broadcasted_iota(jnp.int32, (BLK, DT_PAD), 1)
    mine = (lane_c >= off) & (lane_c < off + H_B)
    w_c = jnp.where(mine, jnp.exp(tot_c - s_c) * dt_c, 0.0)
    w_exp = _dot(_split3(w_c), xexp_ref[...])
    lane_8 = lax.broadcasted_iota(jnp.int32, (8, DT_PAD), 1)
    dec_c = jnp.where((lane_8 >= off) & (lane_8 < off + H_B), jnp.broadcast_to(jnp.exp(tot_c), (8, DT_PAD)), 0.0)
    dec_exp = _dot(_split3(dec_c), xexp_ref[...])[0:1, :]

    xs = xs_ref[...]
    h_in = h_scr[...]
    h_in_b = h_in.astype(BF16)
    lane = lax.broadcasted_iota(jnp.int32, (BLK, BLK), 1)
    first_head = lane < HEAD_P

    for g in range(G_B):
        cg = c_ref[:, g * N_STATE:(g + 1) * N_STATE]
        bg = b_ref[:, g * N_STATE:(g + 1) * N_STATE]
        cb = _dot_nt(cg, bg)
        cg_f = cg.astype(F32)
        for pair in range(R_B // 2):
            cols = slice((g * (R_B // 2) + pair) * 2 * HEAD_P, (g * (R_B // 2) + pair + 1) * 2 * HEAD_P)
            rhs = jnp.concatenate([xs[:, cols], h_in_b[:, cols]], axis=0)
            outs = []
            for q in range(2):
                r = g * R_B + pair * 2 + q
                s_col = jnp.broadcast_to(s_c[:, off + r:off + r + 1], (BLK, BLK))
                seg = s_col - s_r[r:r + 1, :]
                decay = jnp.exp(jnp.where(tri, seg, -jnp.inf))
                m_in = (cb * decay * dt_r[r:r + 1, :]).astype(BF16)
                m_st = (cg_f * jnp.exp(s_col)).astype(BF16)
                outs.append(_dot(jnp.concatenate([m_in, m_st], axis=1), rhs))
            y_scr[:, cols] = jnp.where(first_head, outs[0], outs[1])

    xw = (xs.astype(F32) * w_exp).astype(BF16)
    for g in range(G_B):
        cols = slice(g * R_B * HEAD_P, (g + 1) * R_B * HEAD_P)
        bg = b_ref[:, g * N_STATE:(g + 1) * N_STATE]
        h_scr[:, cols] = h_in[:, cols] * dec_exp[:, cols] + _dot_tn(bg, xw[:, cols])

    @pl.when(step == nc - 1)
    def _():
        hout_ref[0] = h_scr[...]

    if epilogue:
        y = y_scr[...] + yf_ref[...] + dskip_ref[...] * xs.astype(F32)
        y = y * z_ref[...].astype(F32)
        for g in range(G_B):
            cols = slice(g * R_B * HEAD_P, (g + 1) * R_B * HEAD_P)
            y_ref[:, cols] = _rms(y[:, cols], gn_ref[:, cols]).astype(y_ref.dtype)
    else:
        y_ref[...] = y_scr[...]


def _ssd(xbc, steps, xexp, h0, *, nb, seq, direction, epi=None):
    nc = seq // BLK
    m = nb * seq
    dt, cum, dtt, cumt = steps

    def tok(b_, c_):
        return b_ * nc + (c_ if direction == 0 else nc - 1 - c_)

    const2 = lambda b_, c_: (0, 0)
    in_specs = [pl.BlockSpec((BLK, D_INNER), lambda b_, c_: (tok(b_, c_), 0)),
                pl.BlockSpec((BLK, G_B * N_STATE), lambda b_, c_: (tok(b_, c_), D_INNER // (G_B * N_STATE))),
                pl.BlockSpec((BLK, G_B * N_STATE), lambda b_, c_: (tok(b_, c_), D_INNER // (G_B * N_STATE) + 1)),
                pl.BlockSpec((BLK, DT_PAD), lambda b_, c_: (tok(b_, c_), 0)),
                pl.BlockSpec((BLK, DT_PAD), lambda b_, c_: (tok(b_, c_), 0)),
                pl.BlockSpec((2 * H_B, BLK), lambda b_, c_: (0, tok(b_, c_))),
                pl.BlockSpec((2 * H_B, BLK), lambda b_, c_: (0, tok(b_, c_))),
                pl.BlockSpec((3 * DT_PAD, D_INNER), const2),
                pl.BlockSpec((1, N_STATE, D_INNER), lambda b_, c_: (b_, 0, 0))]
    args = [xbc, xbc, xbc, dt, cum, dtt, cumt, xexp[direction], h0]
    if epi is not None:
        y_fwd, proj, d_skip, gn_g = epi
        in_specs += [pl.BlockSpec((BLK, D_INNER), lambda b_, c_: (tok(b_, c_), 0)),
                     pl.BlockSpec((BLK, D_INNER), lambda b_, c_: (tok(b_, c_), 2 * D_A // D_INNER)),
                     pl.BlockSpec((1, D_INNER), const2),
                     pl.BlockSpec((1, D_INNER), const2)]
        args += [y_fwd, proj, d_skip, gn_g]
    return pl.pallas_call(
        functools.partial(_ssd_body, direction=direction, nc=nc, epilogue=epi is not None),
        grid=(nb, nc),
        in_specs=in_specs,
        out_specs=[pl.BlockSpec((BLK, D_INNER), lambda b_, c_: (tok(b_, c_), 0)),
                   pl.BlockSpec((1, N_STATE, D_INNER), lambda b_, c_: (b_, 0, 0))],
        out_shape=[jax.ShapeDtypeStruct((m, D_INNER), BF16 if epi is not None else F32),
                   jax.ShapeDtypeStruct((nb, N_STATE, D_INNER), F32)],
        scratch_shapes=[pltpu.VMEM((N_STATE, D_INNER), F32), pltpu.VMEM((BLK, D_INNER), F32)],
        compiler_params=_params("parallel", "arbitrary"),
        name="ssd_bwd_gate" if epi is not None else "ssd_fwd",
    )(*args)


def _proj_resid_body(a1_ref, a2_ref, w_ref, h_ref, mod_ref, g_ref, o_ref, y_ref, *, k1, gate_idx):
    y_ref[...] = _dot(a1_ref[...], w_ref[:k1, :]) + _dot(a2_ref[...], w_ref[k1:, :])
    _gated_norm_residual_rows(y_ref, h_ref, g_ref, mod_ref, gate_idx, o_ref)


def _proj_resid(a1, a2, w, h, mods, g, *, gate_idx, tm):
    m, k1 = a1.shape
    n = w.shape[1]
    tm = min(tm, m // mods.shape[0])
    rows_per_mod = m // mods.shape[0]
    assert rows_per_mod % tm == 0 and a2.shape == a1.shape and w.shape[0] == 2 * k1
    return pl.pallas_call(
        functools.partial(_proj_resid_body, k1=k1, gate_idx=gate_idx),
        grid=(m // tm,),
        in_specs=[pl.BlockSpec((tm, k1), lambda i: (i, 0)),
                  pl.BlockSpec((tm, k1), lambda i: (i, 0)),
                  pl.BlockSpec((2 * k1, n), lambda i: (0, 0), pipeline_mode=pl.Buffered(1)),
                  pl.BlockSpec((tm, n), lambda i: (i, 0)),
                  pl.BlockSpec((1, N_MOD, n), lambda i: ((i * tm) // rows_per_mod, 0, 0)),
                  pl.BlockSpec((1, n), lambda i: (0, 0))],
        out_specs=pl.BlockSpec((tm, n), lambda i: (i, 0)),
        out_shape=jax.ShapeDtypeStruct((m, n), F32),
        scratch_shapes=[pltpu.VMEM((tm, n), F32)],
        compiler_params=_params("parallel"),
        name="proj_resid",
    )(a1, a2, w, h, mods, g.reshape(1, n))


def _mlp_body(h_ref, mod_ref, gpre_ref, gpost_ref, w1_ref, w2_ref, o_ref, a_ref, acc_ref, *, nf):
    f = pl.program_id(1)

    @pl.when(f == 0)
    def _():
        _norm_modulate_rows(h_ref, gpre_ref, mod_ref, a_ref, 3, 4)

    hid = jnp.square(jnp.maximum(_dot(a_ref[...], w1_ref[...]), 0.0)).astype(BF16)

    @pl.when(f == 0)
    def _():
        acc_ref[...] = _dot(hid, w2_ref[...])

    @pl.when(f > 0)
    def _():
        acc_ref[...] += _dot(hid, w2_ref[...])

    @pl.when(f == nf - 1)
    def _():
        _gated_norm_residual_rows(acc_ref, h_ref, gpost_ref, mod_ref, 5, o_ref)


def _mlp(h, mods, g_pre, g_post, w1, w2, *, tm, tf):
    m, d = h.shape
    ff = w1.shape[1]
    tm = min(tm, m // mods.shape[0])
    rows_per_mod = m // mods.shape[0]
    assert rows_per_mod % tm == 0 and ff % tf == 0
    nf = ff // tf
    return pl.pallas_call(
        functools.partial(_mlp_body, nf=nf),
        grid=(m // tm, nf),
        in_specs=[pl.BlockSpec((tm, d), lambda i, f: (i, 0)),
                  pl.BlockSpec((1, N_MOD, d), lambda i, f: ((i * tm) // rows_per_mod, 0, 0)),
                  pl.BlockSpec((1, d), lambda i, f: (0, 0)),
                  pl.BlockSpec((1, d), lambda i, f: (0, 0)),
                  pl.BlockSpec((d, tf), lambda i, f: (0, f)),
                  pl.BlockSpec((tf, d), lambda i, f: (f, 0))],
        out_specs=pl.BlockSpec((tm, d), lambda i, f: (i, 0)),
        out_shape=jax.ShapeDtypeStruct((m, d), F32),
        scratch_shapes=[pltpu.VMEM((tm, d), BF16), pltpu.VMEM((tm, d), F32)],
        compiler_params=_params("parallel", "arbitrary"),
        name="mlp_sublayer",
    )(h, mods, g_pre.reshape(1, d), g_post.reshape(1, d), w1, w2)


_QC_HEADS = range(0, H_C)
_KC_HEADS = range(H_C, H_C + KV_C)
_QD_HEADS = range(H_C + 2 * KV_C, H_C + 2 * KV_C + H_D)
_KD_HEADS = range(H_C + 2 * KV_C + H_D, H_C + 2 * KV_C + H_D + KV_D)


def _qkv_prep_body(x_ref, cos_ref, sin_ref, qg_ref, kg_ref, o_ref, *, rope):
    lane = lax.broadcasted_iota(jnp.int32, (x_ref.shape[0], HEAD_DIM), 1)
    first_half = (lane % AXIS_DIM) < (AXIS_DIM // 2)
    for hd in range(CD_IN // HEAD_DIM):
        cols = slice(hd * HEAD_DIM, (hd + 1) * HEAD_DIM)
        x = x_ref[:, cols].astype(F32)
        if hd in _QC_HEADS:
            x = _rms(x, qg_ref[...])
        elif hd in _KC_HEADS:
            x = _rms(x, kg_ref[...])
        if rope and (hd in _QC_HEADS or hd in _KC_HEADS or hd in _QD_HEADS or hd in _KD_HEADS):
            partner = jnp.where(first_half,
                                pltpu.roll(x, HEAD_DIM - AXIS_DIM // 2, 1),
                                pltpu.roll(x, AXIS_DIM // 2, 1))
            x = x * cos_ref[...] + partner * sin_ref[...]
        o_ref[:, cols] = x.astype(o_ref.dtype)


def _qkv_prep(proj, cos, sin_signed, q_g, k_g, *, seq, rope, tr):
    m = proj.shape[0]
    tr = min(tr, seq)
    per_seq = seq // tr
    return pl.pallas_call(
        functools.partial(_qkv_prep_body, rope=rope),
        grid=(m // tr,),
        in_specs=[pl.BlockSpec((tr, CD_IN), lambda i: (i, 0)),
                  pl.BlockSpec((tr, HEAD_DIM), lambda i: (i % per_seq, 0)),
                  pl.BlockSpec((tr, HEAD_DIM), lambda i: (i % per_seq, 0)),
                  pl.BlockSpec((1, HEAD_DIM), lambda i: (0, 0)),
                  pl.BlockSpec((1, HEAD_DIM), lambda i: (0, 0))],
        out_specs=pl.BlockSpec((tr, CD_IN), lambda i: (i, 0)),
        out_shape=jax.ShapeDtypeStruct((m, CD_IN), BF16),
        compiler_params=_params("parallel"),
        name="qkv_prep",
    )(proj, cos, sin_signed, q_g.reshape(1, HEAD_DIM), k_g.reshape(1, HEAD_DIM))


def _attn_c_body(q_ref, k_ref, v_ref, kc_ref, vc_ref, o_ref):
    k = k_ref[...]
    v = v_ref[...]
    kc = kc_ref[...]
    vc = vc_ref[...]
    for r in range(H_C // KV_C):
        cols = slice(r * HEAD_DIM, (r + 1) * HEAD_DIM)
        q = q_ref[:, cols]
        s1 = _dot_nt(q, k)
        s2 = _dot_nt(q, kc)
        mx = jnp.maximum(jnp.max(s1, axis=-1, keepdims=True), jnp.max(s2, axis=-1, keepdims=True))
        p1 = jnp.exp2((s1 - mx) * (ATTN_SCALE * LOG2_E))
        p2 = jnp.exp2((s2 - mx) * (ATTN_SCALE * LOG2_E))
        den = jnp.sum(p1, axis=-1, keepdims=True) + jnp.sum(p2, axis=-1, keepdims=True)
        o = _dot(p1.astype(BF16), v) + _dot(p2.astype(BF16), vc)
        o_ref[:, cols] = (o / den).astype(o_ref.dtype)


def _attn_c(qkv, qkv_ctx, *, nb, seq, ctx_len, tq):
    tq = min(tq, seq)
    nq = seq // tq
    rq = H_C // KV_C
    k0 = H_C
    v0 = H_C + KV_C
    return pl.pallas_call(
        _attn_c_body,
        grid=(nb, KV_C, nq),
        in_specs=[pl.BlockSpec((tq, rq * HEAD_DIM), lambda b_, h, i: (b_ * nq + i, h)),
                  pl.BlockSpec((seq, HEAD_DIM), lambda b_, h, i: (b_, k0 + h)),
                  pl.BlockSpec((seq, HEAD_DIM), lambda b_, h, i: (b_, v0 + h)),
                  pl.BlockSpec((ctx_len, HEAD_DIM), lambda b_, h, i: (b_, k0 + h)),
                  pl.BlockSpec((ctx_len, HEAD_DIM), lambda b_, h, i: (b_, v0 + h))],
        out_specs=pl.BlockSpec((tq, rq * HEAD_DIM), lambda b_, h, i: (b_ * nq + i, h)),
        out_shape=jax.ShapeDtypeStruct((nb * seq, H_C * HEAD_DIM), BF16),
        compiler_params=_params("parallel", "parallel", "arbitrary"),
        name="attn_full",
    )(qkv, qkv, qkv, qkv_ctx, qkv_ctx)


def _attn_d_body(q_ref, kp_ref, kn_ref, kx_ref, vp_ref, vn_ref, vx_ref, kc_ref, vc_ref, sink_ref, o_ref, *, ntile):
    t = pl.program_id(2)
    rq = H_D // KV_D
    stack = rq
    tq = q_ref.shape[0]
    nloc = tq + 2 * BLK
    k_all = jnp.concatenate([kp_ref[...], kn_ref[...], kx_ref[...], kc_ref[...]], axis=0)
    v_all = jnp.concatenate([vp_ref[...], vn_ref[...], vx_ref[...], vc_ref[...]], axis=0)

    ii = lax.broadcasted_iota(jnp.int32, (tq, nloc), 0)
    jj = lax.broadcasted_iota(jnp.int32, (tq, nloc), 1)
    d = jj - ii
    lo = jnp.where(t == 0, BLK, 0)
    hi = jnp.where(t == ntile - 1, nloc - BLK, nloc)
    neg = -jnp.inf
    bias = jnp.where(d >= 0, jnp.where(d <= 2 * BLK, jnp.where(jj >= lo, jnp.where(jj < hi, 0.0, neg), neg), neg),
                     neg)
    bias = jnp.concatenate([bias] * stack, axis=0)
    for r0 in range(0, rq, stack):
        heads = range(r0, r0 + stack)
        q = jnp.concatenate([q_ref[:, r * HEAD_DIM:(r + 1) * HEAD_DIM] for r in heads], axis=0)
        s = _dot_nt(q, k_all) * ATTN_SCALE
        s_loc = s[:, :nloc] + bias
        s_ctx = s[:, nloc:]
        sk = jnp.concatenate([jnp.broadcast_to(sink_ref[0, r:r + 1, 0:1], (tq, 1)) for r in heads], axis=0)
        mx = jnp.maximum(jnp.maximum(jnp.max(s_loc, axis=-1, keepdims=True), jnp.max(s_ctx, axis=-1, keepdims=True)),
                         sk)
        p_loc = jnp.exp(s_loc - mx)
        p_ctx = jnp.exp(s_ctx - mx)
        den = jnp.sum(p_loc, axis=-1, keepdims=True) + jnp.sum(p_ctx, axis=-1, keepdims=True) + jnp.exp(sk - mx)
        p = jnp.concatenate([p_loc.astype(BF16), p_ctx.astype(BF16)], axis=1)
        o = _dot(p, v_all) / den
        for n, r in enumerate(heads):
            o_ref[:, r * HEAD_DIM:(r + 1) * HEAD_DIM] = o[n * tq:(n + 1) * tq].astype(o_ref.dtype)


def _attn_d(qkv, qkv_ctx, sink, *, nb, seq, ctx_len):
    nblk = seq // BLK
    tq = 2 * BLK
    ntile = seq // tq
    rq = H_D // KV_D
    q0 = (H_C + 2 * KV_C) // rq
    k0 = H_C + 2 * KV_C + H_D
    v0 = k0 + KV_D

    def edge(col0, shift):
        def index(b_, h, t):
            return (b_ * nblk + jnp.clip(2 * t + shift, 0, nblk - 1), col0 + h)
        return pl.BlockSpec((BLK, HEAD_DIM), index)

    def tile(col0):
        return pl.BlockSpec((tq, HEAD_DIM), lambda b_, h, t: (b_ * ntile + t, col0 + h))

    return pl.pallas_call(
        functools.partial(_attn_d_body, ntile=ntile),
        grid=(nb, KV_D, ntile),
        in_specs=[pl.BlockSpec((tq, rq * HEAD_DIM), lambda b_, h, t: (b_ * ntile + t, q0 + h)),
                  edge(k0, -1), tile(k0), edge(k0, 2),
                  edge(v0, -1), tile(v0), edge(v0, 2),
                  pl.BlockSpec((ctx_len, HEAD_DIM), lambda b_, h, t: (b_, k0 + h)),
                  pl.BlockSpec((ctx_len, HEAD_DIM), lambda b_, h, t: (b_, v0 + h)),
                  pl.BlockSpec((1, rq, BLK), lambda b_, h, t: (h, 0, 0))],
        out_specs=pl.BlockSpec((tq, rq * HEAD_DIM), lambda b_, h, t: (b_ * ntile + t, h)),
        out_shape=jax.ShapeDtypeStruct((nb * seq, H_D * HEAD_DIM), BF16),
        compiler_params=_params("parallel", "parallel", "arbitrary"),
        name="attn_window",
    )(qkv, qkv, qkv, qkv, qkv, qkv, qkv, qkv_ctx, qkv_ctx, sink)


def _rope_tables(seq):
    t = jnp.arange(seq)
    pos = jnp.stack([t // GRID_W, t % GRID_W], axis=-1).astype(F32)
    inv_freq = ROPE_BASE ** (-jnp.arange(0, AXIS_DIM, 2, dtype=F32) / AXIS_DIM)
    ang = pos[:, :, None] * inv_freq
    cos, sin = jnp.cos(ang), jnp.sin(ang)
    cos_t = jnp.concatenate([cos[:, 0], cos[:, 0], cos[:, 1], cos[:, 1]], axis=-1)
    sin_t = jnp.concatenate([-sin[:, 0], sin[:, 0], -sin[:, 1], sin[:, 1]], axis=-1)
    return cos_t, sin_t


def _pad_dt(v):
    return jnp.zeros((1, DT_PAD), F32).at[0, :2 * H_B].set(v.reshape(-1))


def _head_expansion():
    out = []
    head_of_lane = jnp.arange(D_INNER) // HEAD_P
    for direction in range(2):
        rows = jnp.arange(DT_PAD)[:, None] - direction * H_B
        sel = (rows == head_of_lane[None, :]).astype(BF16)
        out.append(jnp.concatenate([sel, sel, sel], axis=0))
    return out


def _ab_layer(h_x, h_c, mods_x, mods_c, norm_g, w_in, w_s, b_s, ln_g, ln_b, conv_w, conv_b, a_log, dt_bias,
              d_skip, gn_g, w_out, w1, w2, *, nb, seq, ctx_len):
    w_main = w_in[:, :AB_MAIN].astype(BF16)
    w_dt = jnp.zeros((D_MODEL, DT_PAD), F32).at[:, :2 * H_B].set(w_in[:, AB_MAIN:]).astype(BF16)
    ws = w_s.astype(BF16)
    bias = jnp.repeat(b_s.T, DH_A, axis=1)
    xexp = _head_expansion()
    b_dt = _pad_dt(dt_bias)
    alog_dt = _pad_dt(a_log)
    dskip = jnp.repeat(d_skip, HEAD_P).reshape(1, D_INNER)
    gn = gn_g.reshape(1, D_INNER)
    w_out_b = w_out.astype(BF16)
    w1_b = w1.astype(BF16)
    w2_b = w2.astype(BF16)

    def project(h, mods):
        proj, dt, cum = _nmm(h, norm_g[0], mods, w_main, w_dt, b_dt, alog_dt, shift_idx=0, scale_idx=1,
                             out_dtype=BF16, tm=1024, tn=1024, act_tiles=(2 * D_A // 1024, D_INNER // 1024))
        return proj, (dt, cum, dt[:, :2 * H_B].T, cum[:, :2 * H_B].T)

    def scans(proj, steps, n_tok, h0):
        xbc = _conv_silu(proj, conv_w, conv_b, nb=nb, seq=n_tok)
        y_f, hf = _ssd(xbc, steps, xexp, h0[0], nb=nb, seq=n_tok, direction=0)
        y_b, hb = _ssd(xbc, steps, xexp, h0[1], nb=nb, seq=n_tok, direction=1, epi=(y_f, proj, dskip, gn))
        return y_b, (hf, hb)

    def finish(h, mods, ya, yb):
        h = _proj_resid(ya, yb, w_out_b, h, mods, norm_g[1], gate_idx=2, tm=512)
        return _mlp(h, mods, norm_g[2], norm_g[3], w1_b, w2_b, tm=512, tf=1024)

    proj_c, steps_c = project(h_c, mods_c)
    proj_x, steps_x = project(h_x, mods_x)
    zero = jnp.zeros((nb, N_STATE, D_INNER), F32)
    yb_c, st_c = scans(proj_c, steps_c, ctx_len, (zero, zero))
    yb_x, _ = scans(proj_x, steps_x, seq, st_c)
    ya_c = _gmlp(proj_c, ws, bias, ln_g, ln_b, tr=512)
    ya_x = _gmlp(proj_x, ws, bias, ln_g, ln_b, tr=512)
    return finish(h_x, mods_x, ya_x, yb_x), finish(h_c, mods_c, ya_c, yb_c)


def _cd_layer(h_x, h_c, mods_x, mods_c, norm_g, w_in, q_g, k_g, sink, w_out, w1, w2, *, nb, seq, ctx_len):
    w_in_b = w_in.astype(BF16)
    cos_t, sin_t = _rope_tables(seq)
    proj_x = _nmm(h_x, norm_g[0], mods_x, w_in_b, shift_idx=0, scale_idx=1, out_dtype=BF16, tm=1024, tn=1024)
    proj_c = _nmm(h_c, norm_g[0], mods_c, w_in_b, shift_idx=0, scale_idx=1, out_dtype=BF16, tm=1024, tn=1024)
    qkv_x = _qkv_prep(proj_x, cos_t, sin_t, q_g, k_g, seq=seq, rope=True, tr=512)
    qkv_c = _qkv_prep(proj_c, cos_t, sin_t, q_g, k_g, seq=ctx_len, rope=False, tr=256)
    oc = _attn_c(qkv_x, qkv_c, nb=nb, seq=seq, ctx_len=ctx_len, tq=512)
    sink_t = jnp.broadcast_to(sink.reshape(KV_D, H_D // KV_D, 1), (KV_D, H_D // KV_D, BLK))
    od = _attn_d(qkv_x, qkv_c, sink_t, nb=nb, seq=seq, ctx_len=ctx_len)
    h = _proj_resid(oc, od, w_out.astype(BF16), h_x, mods_x, norm_g[1], gate_idx=2, tm=512)
    return _mlp(h, mods_x, norm_g[2], norm_g[3], w1.astype(BF16), w2.astype(BF16), tm=512, tf=1024)


def kernel(x, c, ctx, c_ctx, mod_w, mod_b, norm_g, mlp_w1, mlp_w2, ab_w_in, a_w_s, a_b_s, a_ln_g, a_ln_b,
           b_conv_w, b_conv_b, b_a_log, b_dt_bias, b_d, b_norm_g, ab_w_out, cd_w_in, c_q_norm_g, c_k_norm_g,
           d_sink, cd_w_out):
    nb, seq, d = x.shape
    ctx_len = ctx.shape[1]
    depth = mod_w.shape[0]
    assert depth == 2, "the odd layer is implemented as the last layer (no context update)"
    h_x = x.reshape(nb * seq, d)
    h_c = ctx.reshape(nb * ctx_len, d)
    cond_rows = -(-(nb + 1) // 8) * 8
    cond = jnp.zeros((cond_rows, d), F32).at[:nb].set(c).at[nb].set(c_ctx)
    for i in range(depth):
        j = i // 2
        mods = _modvec(cond, mod_w, mod_b, i)
        mods_x = mods[:nb].reshape(nb, N_MOD, d)
        mods_c = mods[nb:nb + 1].reshape(1, N_MOD, d)
        if i % 2 == 0:
            h_x, h_c = _ab_layer(h_x, h_c, mods_x, mods_c, norm_g[i], ab_w_in[j], a_w_s[j], a_b_s[j], a_ln_g[j],
                                 a_ln_b[j], b_conv_w[j], b_conv_b[j], b_a_log[j], b_dt_bias[j], b_d[j],
                                 b_norm_g[j], ab_w_out[j], mlp_w1[i], mlp_w2[i], nb=nb, seq=seq, ctx_len=ctx_len)
        else:
            h_x = _cd_layer(h_x, h_c, mods_x, mods_c, norm_g[i], cd_w_in[j], c_q_norm_g[j], c_k_norm_g[j],
                            d_sink[j], cd_w_out[j], mlp_w1[i], mlp_w2[i], nb=nb, seq=seq, ctx_len=ctx_len)
    return h_x.reshape(nb, seq, d)
```

```python
import functools
import math

import jax
import jax.numpy as jnp
from jax import lax
from jax.experimental import pallas as pl
from jax.experimental.pallas import tpu as pltpu

F32 = jnp.float32
BF16 = jnp.bfloat16

D_MODEL = 2048
GRID_W = 64
BLK = 128
EPS = 1e-6
N_MOD = 6

D_A = 2048
H_A = 8
DH_A = D_A // H_A

D_INNER = 2048
HEAD_P = 64
H_B = D_INNER // HEAD_P
G_B = 4
R_B = H_B // G_B
N_STATE = 128
CONV_W = 5
CONV_DIM = D_INNER + 2 * G_B * N_STATE

HEAD_DIM = 128
H_C = 8
KV_C = 2
H_D = 8
KV_D = 2
ROPE_BASE = 10000.0
AXIS_DIM = HEAD_DIM // 2
D_FF = 4 * D_MODEL
CD_IN = (H_C + 2 * KV_C + H_D + 2 * KV_D) * HEAD_DIM
ATTN_SCALE = HEAD_DIM ** -0.5
LOG2_E = math.log2(math.e)

AB_MAIN = 2 * D_A + D_INNER + CONV_DIM
DT_PAD = 128
SSD_CHUNKS_PER_STEP = 4

VMEM_LIMIT_V7X = 56 * 1024 * 1024


def _params(*sem, vmem=VMEM_LIMIT_V7X):
    return pltpu.CompilerParams(dimension_semantics=sem, vmem_limit_bytes=vmem)


def _rms(x, g):
    return x * lax.rsqrt(jnp.mean(x * x, axis=-1, keepdims=True) + EPS) * g


def _silu(x):
    return x * jax.nn.sigmoid(x)


def _gelu(x):
    return 0.5 * x * (1.0 + lax.erf(x * (1.0 / math.sqrt(2.0))))


def _softplus(x):
    return jnp.maximum(x, 0.0) + jnp.log1p(jnp.exp(-jnp.abs(x)))


def _dot(a, b):
    return jnp.dot(a, b, preferred_element_type=F32)


def _dot_nt(a, b):
    return lax.dot_general(a, b, (((1,), (1,)), ((), ())), preferred_element_type=F32)


def _dot_tn(a, b):
    return lax.dot_general(a, b, (((0,), (0,)), ((), ())), preferred_element_type=F32)


def _chunk_cumsum(x, reverse):
    n = x.shape[0]
    pos = lax.broadcasted_iota(jnp.int32, x.shape, 0) & (BLK - 1)
    k = 1
    while k < BLK:
        if reverse:
            x = x + jnp.where(pos < BLK - k, pltpu.roll(x, n - k, 0), 0.0)
        else:
            x = x + jnp.where(pos >= k, pltpu.roll(x, k, 0), 0.0)
        k *= 2
    return x


def _norm_modulate_rows(x_ref, g_ref, mod_ref, a_ref, shift_idx, scale_idx):
    gs = g_ref[...] * (1.0 + mod_ref[0, scale_idx:scale_idx + 1, :])
    sh = mod_ref[0, shift_idx:shift_idx + 1, :]

    def strip(i, carry):
        r = pl.ds(pl.multiple_of(i * 16, 16), 16)
        x = x_ref[r, :]
        rs = lax.rsqrt(jnp.mean(x * x, axis=-1, keepdims=True) + EPS)
        a_ref[r, :] = (x * rs * gs + sh).astype(a_ref.dtype)
        return carry

    lax.fori_loop(0, x_ref.shape[0] // 16, strip, 0, unroll=8)


def _gated_norm_residual_rows(y_ref, h_ref, g_ref, mod_ref, gate_idx, o_ref):
    gg = mod_ref[0, gate_idx:gate_idx + 1, :] * g_ref[...]

    def strip(i, carry):
        r = pl.ds(pl.multiple_of(i * 8, 8), 8)
        y = y_ref[r, :]
        rs = lax.rsqrt(jnp.mean(y * y, axis=-1, keepdims=True) + EPS)
        o_ref[r, :] = h_ref[r, :] + y * rs * gg
        return carry

    lax.fori_loop(0, y_ref.shape[0] // 8, strip, 0, unroll=16)


def _modvec_body(s_ref, w_ref, b_ref, o_ref):
    a = _silu(s_ref[...]).astype(BF16)
    o_ref[...] = _dot(a, w_ref[...].astype(BF16)) + b_ref[...]


def _modvec(s, w, b, layer):
    r, d = s.shape
    depth, _, n = w.shape
    tn = 1024
    return pl.pallas_call(
        _modvec_body,
        grid=(n // tn,),
        in_specs=[pl.BlockSpec((r, d), lambda j: (0, 0)),
                  pl.BlockSpec((None, d, tn), lambda j: (layer, 0, j)),
                  pl.BlockSpec((None, 1, tn), lambda j: (layer, 0, j))],
        out_specs=pl.BlockSpec((r, tn), lambda j: (0, j)),
        out_shape=jax.ShapeDtypeStruct((r, n), F32),
        compiler_params=_params("arbitrary"),
        name="modvec",
    )(s, w, b.reshape(depth, 1, n))


def _nmm_body(*refs, shift_idx, scale_idx, side, act_tiles):
    if side:
        x_ref, g_ref, mod_ref, w_ref, ws_ref, bs_ref, alog_ref, o_ref, dt_ref, cum_ref, a_ref = refs
    else:
        x_ref, g_ref, mod_ref, w_ref, o_ref, a_ref = refs

    @pl.when(pl.program_id(1) == 0)
    def _():
        _norm_modulate_rows(x_ref, g_ref, mod_ref, a_ref, shift_idx, scale_idx)
        if side:
            dt = _softplus(_dot(a_ref[...], ws_ref[...]) + bs_ref[...])
            dt_ref[...] = dt
            dta = dt * (-jnp.exp(alog_ref[...]))
            lane = lax.broadcasted_iota(jnp.int32, dta.shape, 1)
            cum_ref[...] = jnp.where(lane < H_B, _chunk_cumsum(dta, False), _chunk_cumsum(dta, True))

    if act_tiles is None:
        o_ref[...] = _dot(a_ref[...], w_ref[...]).astype(o_ref.dtype)
    else:
        j = pl.program_id(1)
        n_gelu, n_silu = act_tiles

        @pl.when(j < n_gelu)
        def _():
            o_ref[...] = _gelu(_dot(a_ref[...], w_ref[...])).astype(o_ref.dtype)

        @pl.when((j >= n_gelu) & (j < n_gelu + n_silu))
        def _():
            o_ref[...] = _silu(_dot(a_ref[...], w_ref[...])).astype(o_ref.dtype)

        @pl.when(j >= n_gelu + n_silu)
        def _():
            o_ref[...] = _dot(a_ref[...], w_ref[...]).astype(o_ref.dtype)


def _nmm(x, g, mods, w, w_side=None, b_side=None, alog_side=None, *, shift_idx, scale_idx, out_dtype, tm, tn,
         act_tiles=None, n_cols=None):
    m, k = x.shape
    n = w.shape[1] if n_cols is None else n_cols
    tm = min(tm, m // mods.shape[0])
    tn = min(tn, n)
    rows_per_mod = m // mods.shape[0]
    assert rows_per_mod % tm == 0 and n % tn == 0 and m % tm == 0
    side = w_side is not None
    in_specs = [pl.BlockSpec((tm, k), lambda i, j: (i, 0)),
                pl.BlockSpec((1, k), lambda i, j: (0, 0)),
                pl.BlockSpec((1, N_MOD, k), lambda i, j: ((i * tm) // rows_per_mod, 0, 0)),
                pl.BlockSpec((k, tn), lambda i, j: (0, j))]
    out_specs = [pl.BlockSpec((tm, tn), lambda i, j: (i, j))]
    out_shape = [jax.ShapeDtypeStruct((m, n), out_dtype)]
    args = [x, g.reshape(1, k), mods, w]
    if side:
        ns = w_side.shape[1]
        in_specs += [pl.BlockSpec((k, ns), lambda i, j: (0, 0)), pl.BlockSpec((1, ns), lambda i, j: (0, 0)),
                     pl.BlockSpec((1, ns), lambda i, j: (0, 0))]
        out_specs += [pl.BlockSpec((tm, ns), lambda i, j: (i, 0))] * 2
        out_shape += [jax.ShapeDtypeStruct((m, ns), F32)] * 2
        args += [w_side, b_side, alog_side]
        assert tm % BLK == 0
    out = pl.pallas_call(
        functools.partial(_nmm_body, shift_idx=shift_idx, scale_idx=scale_idx, side=side, act_tiles=act_tiles),
        grid=(m // tm, n // tn),
        in_specs=in_specs,
        out_specs=out_specs,
        out_shape=out_shape,
        scratch_shapes=[pltpu.VMEM((tm, k), BF16)],
        compiler_params=_params("parallel", "arbitrary"),
        name="norm_mod_matmul",
    )(*args)
    return out if side else out[0]


def _gmlp_body(u_ref, v_ref, lng_ref, lnb_ref, ws_ref, bias_ref, o_ref, *, nchunk):
    v = v_ref[...].astype(F32)
    mu = jnp.mean(v, axis=-1, keepdims=True)
    vc = v - mu
    vn = vc * lax.rsqrt(jnp.mean(vc * vc, axis=-1, keepdims=True) + EPS) * lng_ref[...] + lnb_ref[...]
    vb = vn.astype(BF16)
    for c in range(nchunk):
        rows = slice(c * BLK, (c + 1) * BLK)
        for h in range(H_A):
            cols = slice(h * DH_A, (h + 1) * DH_A)
            mix = _dot(ws_ref[h], vb[rows, cols]) + bias_ref[:, cols]
            u = u_ref[rows, cols].astype(F32)
            o_ref[rows, cols] = (u * mix).astype(o_ref.dtype)


def _gmlp(proj, ws, bias, ln_g, ln_b, *, tr):
    m = proj.shape[0]
    tr = min(tr, m)
    return pl.pallas_call(
        functools.partial(_gmlp_body, nchunk=tr // BLK),
        grid=(m // tr,),
        in_specs=[pl.BlockSpec((tr, D_A), lambda i: (i, 0)),
                  pl.BlockSpec((tr, D_A), lambda i: (i, 1)),
                  pl.BlockSpec((1, D_A), lambda i: (0, 0)),
                  pl.BlockSpec((1, D_A), lambda i: (0, 0)),
                  pl.BlockSpec((H_A, BLK, BLK), lambda i: (0, 0, 0)),
                  pl.BlockSpec((BLK, D_A), lambda i: (0, 0))],
        out_specs=pl.BlockSpec((tr, D_A), lambda i: (i, 0)),
        out_shape=jax.ShapeDtypeStruct((m, D_A), BF16),
        compiler_params=_params("parallel"),
        name="gmlp_gate",
    )(proj, proj, ln_g.reshape(1, D_A), ln_b.reshape(1, D_A), ws, bias)


def _conv_body(x_ref, w_ref, b_ref, o_ref, *, seq):
    x = x_ref[...].astype(F32)
    row = lax.broadcasted_iota(jnp.int32, x.shape, 0)
    acc = x * w_ref[CONV_W // 2:CONV_W // 2 + 1, :] + b_ref[...]
    for k in range(CONV_W):
        d = k - CONV_W // 2
        if d == 0:
            continue
        shifted = pltpu.roll(x, (-d) % seq, 0)
        valid = (row >= -d) if d < 0 else (row < seq - d)
        acc = acc + jnp.where(valid, shifted, 0.0) * w_ref[k:k + 1, :]
    o_ref[...] = _silu(acc).astype(o_ref.dtype)


def _conv_silu(proj, w, b, *, nb, seq):
    tc = 512
    col0 = (AB_MAIN - CONV_DIM) // tc
    return pl.pallas_call(
        functools.partial(_conv_body, seq=seq),
        grid=(nb, CONV_DIM // tc),
        in_specs=[pl.BlockSpec((seq, tc), lambda b_, j: (b_, col0 + j)),
                  pl.BlockSpec((CONV_W, tc), lambda b_, j: (0, j)),
                  pl.BlockSpec((1, tc), lambda b_, j: (0, j))],
        out_specs=pl.BlockSpec((seq, tc), lambda b_, j: (b_, j)),
        out_shape=jax.ShapeDtypeStruct((nb * seq, CONV_DIM), BF16),
        compiler_params=_params("parallel", "parallel"),
        name="conv_silu",
    )(proj, w, b.reshape(1, CONV_DIM))


def _split3(v):
    hi = v.astype(BF16)
    r1 = v - hi.astype(F32)
    mid = r1.astype(BF16)
    lo = (r1 - mid.astype(F32)).astype(BF16)
    return jnp.concatenate([hi, mid, lo], axis=1)


def _ssd_body(*refs, direction, nsteps, sub, epilogue):
    if epilogue:
        (xs_ref, b_ref, c_ref, dt_ref, cum_ref, dtt_ref, cumt_ref, xexp_ref, h0_ref,
         yf_ref, z_ref, dskip_ref, gn_ref, y_ref, hout_ref, h_scr, y_scr) = refs
    else:
        (xs_ref, b_ref, c_ref, dt_ref, cum_ref, dtt_ref, cumt_ref, xexp_ref, h0_ref,
         y_ref, hout_ref, h_scr) = refs
    step = pl.program_id(1)

    @pl.when(step == 0)
    def _():
        h_scr[...] = h0_ref[0]

    off = direction * H_B
    ii = lax.broadcasted_iota(jnp.int32, (BLK, BLK), 0)
    jj = lax.broadcasted_iota(jnp.int32, (BLK, BLK), 1)
    tri = (jj <= ii) if direction == 0 else (jj >= ii)
    last = BLK - 1 if direction == 0 else 0
    lane_c = lax.broadcasted_iota(jnp.int32, (BLK, DT_PAD), 1)
    mine = (lane_c >= off) & (lane_c < off + H_B)
    lane_8 = lax.broadcasted_iota(jnp.int32, (8, DT_PAD), 1)
    mine_8 = (lane_8 >= off) & (lane_8 < off + H_B)
    first_head = jj < HEAD_P

    def chunk(k):
        rows = slice(k * BLK, (k + 1) * BLK)
        dt_c = dt_ref[rows, :]
        s_c = cum_ref[rows, :]
        dt_r = dtt_ref[off:off + H_B, rows]
        s_r = cumt_ref[off:off + H_B, rows]
        tot_c = s_c[last:last + 1, :]

        w_c = jnp.where(mine, jnp.exp(tot_c - s_c) * dt_c, 0.0)
        w_exp = _dot(_split3(w_c), xexp_ref[...])
        dec_c = jnp.where(mine_8, jnp.broadcast_to(jnp.exp(tot_c), (8, DT_PAD)), 0.0)
        dec_exp = _dot(_split3(dec_c), xexp_ref[...])[0:1, :]

        xs = xs_ref[rows, :]
        h_in = h_scr[...]
        h_in_b = h_in.astype(BF16)
        y_dst = y_scr if epilogue else y_ref.at[rows, :]

        for g in range(G_B):
            cg = c_ref[rows, g * N_STATE:(g + 1) * N_STATE]
            bg = b_ref[rows, g * N_STATE:(g + 1) * N_STATE]
            cb = _dot_nt(cg, bg)
            cg_f = cg.astype(F32)
            for pair in range(R_B // 2):
                cols = slice((g * (R_B // 2) + pair) * 2 * HEAD_P, (g * (R_B // 2) + pair + 1) * 2 * HEAD_P)
                rhs = jnp.concatenate([xs[:, cols], h_in_b[:, cols]], axis=0)
                outs = []
                for q in range(2):
                    r = g * R_B + pair * 2 + q
                    s_col = jnp.broadcast_to(s_c[:, off + r:off + r + 1], (BLK, BLK))
                    seg = s_col - s_r[r:r + 1, :]
                    decay = jnp.exp(jnp.where(tri, seg, -jnp.inf))
                    m_in = (cb * decay * dt_r[r:r + 1, :]).astype(BF16)
                    m_st = (cg_f * jnp.exp(s_col)).astype(BF16)
                    outs.append(_dot(jnp.concatenate([m_in, m_st], axis=1), rhs))
                y_dst[:, cols] = jnp.where(first_head, outs[0], outs[1])

        xw = (xs.astype(F32) * w_exp).astype(BF16)
        for g in range(G_B):
            cols = slice(g * R_B * HEAD_P, (g + 1) * R_B * HEAD_P)
            bg = b_ref[rows, g * N_STATE:(g + 1) * N_STATE]
            h_scr[:, cols] = h_in[:, cols] * dec_exp[:, cols] + _dot_tn(bg, xw[:, cols])

        if epilogue:
            y = y_scr[...] + yf_ref[rows, :] + dskip_ref[...] * xs.astype(F32)
            y = y * z_ref[rows, :].astype(F32)
            for g in range(G_B):
                cols = slice(g * R_B * HEAD_P, (g + 1) * R_B * HEAD_P)
                y_ref[rows, cols] = _rms(y[:, cols], gn_ref[:, cols]).astype(y_ref.dtype)

    for k in (range(sub) if direction == 0 else reversed(range(sub))):
        chunk(k)

    @pl.when(step == nsteps - 1)
    def _():
        hout_ref[0] = h_scr[...]


def _ssd(xbc, steps, xexp, h0, *, nb, seq, direction, epi=None):
    sub = min(SSD_CHUNKS_PER_STEP, seq // BLK)
    tb = sub * BLK
    assert seq % tb == 0
    ns = seq // tb
    m = nb * seq
    dt, cum, dtt, cumt = steps

    def tok(b_, c_):
        return b_ * ns + (c_ if direction == 0 else ns - 1 - c_)

    const2 = lambda b_, c_: (0, 0)
    in_specs = [pl.BlockSpec((tb, D_INNER), lambda b_, c_: (tok(b_, c_), 0)),
                pl.BlockSpec((tb, G_B * N_STATE), lambda b_, c_: (tok(b_, c_), D_INNER // (G_B * N_STATE))),
                pl.BlockSpec((tb, G_B * N_STATE), lambda b_, c_: (tok(b_, c_), D_INNER // (G_B * N_STATE) + 1)),
                pl.BlockSpec((tb, DT_PAD), lambda b_, c_: (tok(b_, c_), 0)),
                pl.BlockSpec((tb, DT_PAD), lambda b_, c_: (tok(b_, c_), 0)),
                pl.BlockSpec((2 * H_B, tb), lambda b_, c_: (0, tok(b_, c_))),
                pl.BlockSpec((2 * H_B, tb), lambda b_, c_: (0, tok(b_, c_))),
                pl.BlockSpec((3 * DT_PAD, D_INNER), const2),
                pl.BlockSpec((1, N_STATE, D_INNER), lambda b_, c_: (b_, 0, 0))]
    args = [xbc, xbc, xbc, dt, cum, dtt, cumt, xexp[direction], h0]
    scratch = [pltpu.VMEM((N_STATE, D_INNER), F32)]
    if epi is not None:
        y_fwd, proj, d_skip, gn_g = epi
        in_specs += [pl.BlockSpec((tb, D_INNER), lambda b_, c_: (tok(b_, c_), 0)),
                     pl.BlockSpec((tb, D_INNER), lambda b_, c_: (tok(b_, c_), 2 * D_A // D_INNER)),
                     pl.BlockSpec((1, D_INNER), const2),
                     pl.BlockSpec((1, D_INNER), const2)]
        args += [y_fwd, proj, d_skip, gn_g]
        scratch.append(pltpu.VMEM((BLK, D_INNER), F32))
    return pl.pallas_call(
        functools.partial(_ssd_body, direction=direction, nsteps=ns, sub=sub, epilogue=epi is not None),
        grid=(nb, ns),
        in_specs=in_specs,
        out_specs=[pl.BlockSpec((tb, D_INNER), lambda b_, c_: (tok(b_, c_), 0)),
                   pl.BlockSpec((1, N_STATE, D_INNER), lambda b_, c_: (b_, 0, 0))],
        out_shape=[jax.ShapeDtypeStruct((m, D_INNER), BF16 if epi is not None else F32),
                   jax.ShapeDtypeStruct((nb, N_STATE, D_INNER), F32)],
        scratch_shapes=scratch,
        compiler_params=_params("parallel", "arbitrary"),
        name="ssd_bwd_gate" if epi is not None else "ssd_fwd",
    )(*args)


def _proj_resid_body(a1_ref, a2_ref, w_ref, h_ref, mod_ref, g_ref, o_ref, y_ref, *, k1, gate_idx):
    y_ref[...] = _dot(a1_ref[...], w_ref[:k1, :]) + _dot(a2_ref[...], w_ref[k1:, :])
    _gated_norm_residual_rows(y_ref, h_ref, g_ref, mod_ref, gate_idx, o_ref)


def _proj_resid(a1, a2, w, h, mods, g, *, gate_idx, tm):
    m, k1 = a1.shape
    n = w.shape[1]
    tm = min(tm, m // mods.shape[0])
    rows_per_mod = m // mods.shape[0]
    assert rows_per_mod % tm == 0 and a2.shape == a1.shape and w.shape[0] == 2 * k1
    return pl.pallas_call(
        functools.partial(_proj_resid_body, k1=k1, gate_idx=gate_idx),
        grid=(m // tm,),
        in_specs=[pl.BlockSpec((tm, k1), lambda i: (i, 0)),
                  pl.BlockSpec((tm, k1), lambda i: (i, 0)),
                  pl.BlockSpec((2 * k1, n), lambda i: (0, 0), pipeline_mode=pl.Buffered(1)),
                  pl.BlockSpec((tm, n), lambda i: (i, 0)),
                  pl.BlockSpec((1, N_MOD, n), lambda i: ((i * tm) // rows_per_mod, 0, 0)),
                  pl.BlockSpec((1, n), lambda i: (0, 0))],
        out_specs=pl.BlockSpec((tm, n), lambda i: (i, 0)),
        out_shape=jax.ShapeDtypeStruct((m, n), F32),
        scratch_shapes=[pltpu.VMEM((tm, n), F32)],
        compiler_params=_params("parallel"),
        name="proj_resid",
    )(a1, a2, w, h, mods, g.reshape(1, n))


def _mlp_body(h_ref, mod_ref, gpre_ref, gpost_ref, w1_ref, w2_ref, o_ref, a_ref, acc_ref, *, nf):
    f = pl.program_id(1)

    @pl.when(f == 0)
    def _():
        _norm_modulate_rows(h_ref, gpre_ref, mod_ref, a_ref, 3, 4)

    hid = jnp.square(jnp.maximum(_dot(a_ref[...], w1_ref[...]), 0.0)).astype(BF16)

    @pl.when(f == 0)
    def _():
        acc_ref[...] = _dot(hid, w2_ref[...])

    @pl.when(f > 0)
    def _():
        acc_ref[...] += _dot(hid, w2_ref[...])

    @pl.when(f == nf - 1)
    def _():
        _gated_norm_residual_rows(acc_ref, h_ref, gpost_ref, mod_ref, 5, o_ref)


def _mlp(h, mods, g_pre, g_post, w1, w2, *, tm, tf):
    m, d = h.shape
    ff = w1.shape[1]
    tm = min(tm, m // mods.shape[0])
    rows_per_mod = m // mods.shape[0]
    assert rows_per_mod % tm == 0 and ff % tf == 0
    nf = ff // tf
    return pl.pallas_call(
        functools.partial(_mlp_body, nf=nf),
        grid=(m // tm, nf),
        in_specs=[pl.BlockSpec((tm, d), lambda i, f: (i, 0)),
                  pl.BlockSpec((1, N_MOD, d), lambda i, f: ((i * tm) // rows_per_mod, 0, 0)),
                  pl.BlockSpec((1, d), lambda i, f: (0, 0)),
                  pl.BlockSpec((1, d), lambda i, f: (0, 0)),
                  pl.BlockSpec((d, tf), lambda i, f: (0, f)),
                  pl.BlockSpec((tf, d), lambda i, f: (f, 0))],
        out_specs=pl.BlockSpec((tm, d), lambda i, f: (i, 0)),
        out_shape=jax.ShapeDtypeStruct((m, d), F32),
        scratch_shapes=[pltpu.VMEM((tm, d), BF16), pltpu.VMEM((tm, d), F32)],
        compiler_params=_params("parallel", "arbitrary"),
        name="mlp_sublayer",
    )(h, mods, g_pre.reshape(1, d), g_post.reshape(1, d), w1, w2)


_QC_HEADS = range(0, H_C)
_KC_HEADS = range(H_C, H_C + KV_C)
_QD_HEADS = range(H_C + 2 * KV_C, H_C + 2 * KV_C + H_D)
_KD_HEADS = range(H_C + 2 * KV_C + H_D, H_C + 2 * KV_C + H_D + KV_D)


def _qkv_prep_body(x_ref, cos_ref, sin_ref, qg_ref, kg_ref, o_ref, *, rope):
    lane = lax.broadcasted_iota(jnp.int32, (x_ref.shape[0], HEAD_DIM), 1)
    first_half = (lane % AXIS_DIM) < (AXIS_DIM // 2)
    for hd in range(CD_IN // HEAD_DIM):
        cols = slice(hd * HEAD_DIM, (hd + 1) * HEAD_DIM)
        x = x_ref[:, cols].astype(F32)
        if hd in _QC_HEADS:
            x = _rms(x, qg_ref[...])
        elif hd in _KC_HEADS:
            x = _rms(x, kg_ref[...])
        if rope and (hd in _QC_HEADS or hd in _KC_HEADS or hd in _QD_HEADS or hd in _KD_HEADS):
            partner = jnp.where(first_half,
                                pltpu.roll(x, HEAD_DIM - AXIS_DIM // 2, 1),
                                pltpu.roll(x, AXIS_DIM // 2, 1))
            x = x * cos_ref[...] + partner * sin_ref[...]
        o_ref[:, cols] = x.astype(o_ref.dtype)


def _qkv_prep(proj, cos, sin_signed, q_g, k_g, *, seq, rope, tr):
    m = proj.shape[0]
    tr = min(tr, seq)
    per_seq = seq // tr
    return pl.pallas_call(
        functools.partial(_qkv_prep_body, rope=rope),
        grid=(m // tr,),
        in_specs=[pl.BlockSpec((tr, CD_IN), lambda i: (i, 0)),
                  pl.BlockSpec((tr, HEAD_DIM), lambda i: (i % per_seq, 0)),
                  pl.BlockSpec((tr, HEAD_DIM), lambda i: (i % per_seq, 0)),
                  pl.BlockSpec((1, HEAD_DIM), lambda i: (0, 0)),
                  pl.BlockSpec((1, HEAD_DIM), lambda i: (0, 0))],
        out_specs=pl.BlockSpec((tr, CD_IN), lambda i: (i, 0)),
        out_shape=jax.ShapeDtypeStruct((m, CD_IN), BF16),
        compiler_params=_params("parallel"),
        name="qkv_prep",
    )(proj, cos, sin_signed, q_g.reshape(1, HEAD_DIM), k_g.reshape(1, HEAD_DIM))


def _attn_c_body(q_ref, k_ref, v_ref, kc_ref, vc_ref, o_ref):
    k = k_ref[...]
    v = v_ref[...]
    kc = kc_ref[...]
    vc = vc_ref[...]
    for r in range(H_C // KV_C):
        cols = slice(r * HEAD_DIM, (r + 1) * HEAD_DIM)
        q = q_ref[:, cols]
        s1 = _dot_nt(q, k)
        s2 = _dot_nt(q, kc)
        mx = jnp.maximum(jnp.max(s1, axis=-1, keepdims=True), jnp.max(s2, axis=-1, keepdims=True))
        p1 = jnp.exp2((s1 - mx) * (ATTN_SCALE * LOG2_E))
        p2 = jnp.exp2((s2 - mx) * (ATTN_SCALE * LOG2_E))
        den = jnp.sum(p1, axis=-1, keepdims=True) + jnp.sum(p2, axis=-1, keepdims=True)
        o = _dot(p1.astype(BF16), v) + _dot(p2.astype(BF16), vc)
        o_ref[:, cols] = (o / den).astype(o_ref.dtype)


def _attn_c(qkv, qkv_ctx, *, nb, seq, ctx_len, tq):
    tq = min(tq, seq)
    nq = seq // tq
    rq = H_C // KV_C
    k0 = H_C
    v0 = H_C + KV_C
    return pl.pallas_call(
        _attn_c_body,
        grid=(nb, KV_C, nq),
        in_specs=[pl.BlockSpec((tq, rq * HEAD_DIM), lambda b_, h, i: (b_ * nq + i, h)),
                  pl.BlockSpec((seq, HEAD_DIM), lambda b_, h, i: (b_, k0 + h)),
                  pl.BlockSpec((seq, HEAD_DIM), lambda b_, h, i: (b_, v0 + h)),
                  pl.BlockSpec((ctx_len, HEAD_DIM), lambda b_, h, i: (b_, k0 + h)),
                  pl.BlockSpec((ctx_len, HEAD_DIM), lambda b_, h, i: (b_, v0 + h))],
        out_specs=pl.BlockSpec((tq, rq * HEAD_DIM), lambda b_, h, i: (b_ * nq + i, h)),
        out_shape=jax.ShapeDtypeStruct((nb * seq, H_C * HEAD_DIM), BF16),
        compiler_params=_params("parallel", "parallel", "arbitrary"),
        name="attn_full",
    )(qkv, qkv, qkv, qkv_ctx, qkv_ctx)


def _attn_d_body(q_ref, kp_ref, kn_ref, kx_ref, vp_ref, vn_ref, vx_ref, kc_ref, vc_ref, sink_ref, o_ref, *, ntile):
    t = pl.program_id(2)
    rq = H_D // KV_D
    stack = rq
    tq = q_ref.shape[0]
    nloc = tq + 2 * BLK
    k_all = jnp.concatenate([kp_ref[...], kn_ref[...], kx_ref[...], kc_ref[...]], axis=0)
    v_all = jnp.concatenate([vp_ref[...], vn_ref[...], vx_ref[...], vc_ref[...]], axis=0)

    ii = lax.broadcasted_iota(jnp.int32, (tq, nloc), 0)
    jj = lax.broadcasted_iota(jnp.int32, (tq, nloc), 1)
    d = jj - ii
    lo = jnp.where(t == 0, BLK, 0)
    hi = jnp.where(t == ntile - 1, nloc - BLK, nloc)
    neg = -jnp.inf
    bias = jnp.where(d >= 0, jnp.where(d <= 2 * BLK, jnp.where(jj >= lo, jnp.where(jj < hi, 0.0, neg), neg), neg),
                     neg)
    bias = jnp.concatenate([bias] * stack, axis=0)
    for r0 in range(0, rq, stack):
        heads = range(r0, r0 + stack)
        q = jnp.concatenate([q_ref[:, r * HEAD_DIM:(r + 1) * HEAD_DIM] for r in heads], axis=0)
        s = _dot_nt(q, k_all) * ATTN_SCALE
        s_loc = s[:, :nloc] + bias
        s_ctx = s[:, nloc:]
        sk = jnp.concatenate([jnp.broadcast_to(sink_ref[0, r:r + 1, 0:1], (tq, 1)) for r in heads], axis=0)
        mx = jnp.maximum(jnp.maximum(jnp.max(s_loc, axis=-1, keepdims=True), jnp.max(s_ctx, axis=-1, keepdims=True)),
                         sk)
        p_loc = jnp.exp(s_loc - mx)
        p_ctx = jnp.exp(s_ctx - mx)
        den = jnp.sum(p_loc, axis=-1, keepdims=True) + jnp.sum(p_ctx, axis=-1, keepdims=True) + jnp.exp(sk - mx)
        p = jnp.concatenate([p_loc.astype(BF16), p_ctx.astype(BF16)], axis=1)
        o = _dot(p, v_all) / den
        for n, r in enumerate(heads):
            o_ref[:, r * HEAD_DIM:(r + 1) * HEAD_DIM] = o[n * tq:(n + 1) * tq].astype(o_ref.dtype)


def _attn_d(qkv, qkv_ctx, sink, *, nb, seq, ctx_len):
    nblk = seq // BLK
    tq = 2 * BLK
    ntile = seq // tq
    rq = H_D // KV_D
    q0 = (H_C + 2 * KV_C) // rq
    k0 = H_C + 2 * KV_C + H_D
    v0 = k0 + KV_D

    def edge(col0, shift):
        def index(b_, h, t):
            return (b_ * nblk + jnp.clip(2 * t + shift, 0, nblk - 1), col0 + h)
        return pl.BlockSpec((BLK, HEAD_DIM), index)

    def tile(col0):
        return pl.BlockSpec((tq, HEAD_DIM), lambda b_, h, t: (b_ * ntile + t, col0 + h))

    return pl.pallas_call(
        functools.partial(_attn_d_body, ntile=ntile),
        grid=(nb, KV_D, ntile),
        in_specs=[pl.BlockSpec((tq, rq * HEAD_DIM), lambda b_, h, t: (b_ * ntile + t, q0 + h)),
                  edge(k0, -1), tile(k0), edge(k0, 2),
                  edge(v0, -1), tile(v0), edge(v0, 2),
                  pl.BlockSpec((ctx_len, HEAD_DIM), lambda b_, h, t: (b_, k0 + h)),
                  pl.BlockSpec((ctx_len, HEAD_DIM), lambda b_, h, t: (b_, v0 + h)),
                  pl.BlockSpec((1, rq, BLK), lambda b_, h, t: (h, 0, 0))],
        out_specs=pl.BlockSpec((tq, rq * HEAD_DIM), lambda b_, h, t: (b_ * ntile + t, h)),
        out_shape=jax.ShapeDtypeStruct((nb * seq, H_D * HEAD_DIM), BF16),
        compiler_params=_params("parallel", "parallel", "arbitrary"),
        name="attn_window",
    )(qkv, qkv, qkv, qkv, qkv, qkv, qkv, qkv_ctx, qkv_ctx, sink)


def _rope_tables(seq):
    t = jnp.arange(seq)
    pos = jnp.stack([t // GRID_W, t % GRID_W], axis=-1).astype(F32)
    inv_freq = ROPE_BASE ** (-jnp.arange(0, AXIS_DIM, 2, dtype=F32) / AXIS_DIM)
    ang = pos[:, :, None] * inv_freq
    cos, sin = jnp.cos(ang), jnp.sin(ang)
    cos_t = jnp.concatenate([cos[:, 0], cos[:, 0], cos[:, 1], cos[:, 1]], axis=-1)
    sin_t = jnp.concatenate([-sin[:, 0], sin[:, 0], -sin[:, 1], sin[:, 1]], axis=-1)
    return cos_t, sin_t


def _pad_dt(v):
    return jnp.zeros((1, DT_PAD), F32).at[0, :2 * H_B].set(v.reshape(-1))


def _head_expansion():
    out = []
    head_of_lane = jnp.arange(D_INNER) // HEAD_P
    for direction in range(2):
        rows = jnp.arange(DT_PAD)[:, None] - direction * H_B
        sel = (rows == head_of_lane[None, :]).astype(BF16)
        out.append(jnp.concatenate([sel, sel, sel], axis=0))
    return out


def _ab_layer(h_x, h_c, mods_x, mods_c, norm_g, w_in, w_s, b_s, ln_g, ln_b, conv_w, conv_b, a_log, dt_bias,
              d_skip, gn_g, w_out, w1, w2, *, nb, seq, ctx_len):
    w_main = w_in.astype(BF16)
    w_dt = jnp.zeros((D_MODEL, DT_PAD), F32).at[:, :2 * H_B].set(w_in[:, AB_MAIN:]).astype(BF16)
    ws = w_s.astype(BF16)
    bias = jnp.repeat(b_s.T, DH_A, axis=1)
    xexp = _head_expansion()
    b_dt = _pad_dt(dt_bias)
    alog_dt = _pad_dt(a_log)
    dskip = jnp.repeat(d_skip, HEAD_P).reshape(1, D_INNER)
    gn = gn_g.reshape(1, D_INNER)
    w_out_b = w_out.astype(BF16)
    w1_b = w1.astype(BF16)
    w2_b = w2.astype(BF16)

    def project(h, mods):
        proj, dt, cum = _nmm(h, norm_g[0], mods, w_main, w_dt, b_dt, alog_dt, shift_idx=0, scale_idx=1,
                             out_dtype=BF16, tm=1024, tn=1024, act_tiles=(2 * D_A // 1024, D_INNER // 1024),
                             n_cols=AB_MAIN)
        return proj, (dt, cum, dt[:, :2 * H_B].T, cum[:, :2 * H_B].T)

    def scans(proj, steps, n_tok, h0):
        xbc = _conv_silu(proj, conv_w, conv_b, nb=nb, seq=n_tok)
        y_f, hf = _ssd(xbc, steps, xexp, h0[0], nb=nb, seq=n_tok, direction=0)
        y_b, hb = _ssd(xbc, steps, xexp, h0[1], nb=nb, seq=n_tok, direction=1, epi=(y_f, proj, dskip, gn))
        return y_b, (hf, hb)

    def finish(h, mods, ya, yb):
        h = _proj_resid(ya, yb, w_out_b, h, mods, norm_g[1], gate_idx=2, tm=512)
        return _mlp(h, mods, norm_g[2], norm_g[3], w1_b, w2_b, tm=512, tf=1024)

    proj_c, steps_c = project(h_c, mods_c)
    proj_x, steps_x = project(h_x, mods_x)
    zero = jnp.zeros((nb, N_STATE, D_INNER), F32)
    yb_c, st_c = scans(proj_c, steps_c, ctx_len, (zero, zero))
    yb_x, _ = scans(proj_x, steps_x, seq, st_c)
    ya_c = _gmlp(proj_c, ws, bias, ln_g, ln_b, tr=512)
    ya_x = _gmlp(proj_x, ws, bias, ln_g, ln_b, tr=512)
    return finish(h_x, mods_x, ya_x, yb_x), finish(h_c, mods_c, ya_c, yb_c)


def _cd_layer(h_x, h_c, mods_x, mods_c, norm_g, w_in, q_g, k_g, sink, w_out, w1, w2, *, nb, seq, ctx_len):
    w_in_b = w_in.astype(BF16)
    cos_t, sin_t = _rope_tables(seq)
    proj_x = _nmm(h_x, norm_g[0], mods_x, w_in_b, shift_idx=0, scale_idx=1, out_dtype=BF16, tm=1024, tn=1024)
    proj_c = _nmm(h_c, norm_g[0], mods_c, w_in_b, shift_idx=0, scale_idx=1, out_dtype=BF16, tm=1024, tn=1024)
    qkv_x = _qkv_prep(proj_x, cos_t, sin_t, q_g, k_g, seq=seq, rope=True, tr=512)
    qkv_c = _qkv_prep(proj_c, cos_t, sin_t, q_g, k_g, seq=ctx_len, rope=False, tr=256)
    oc = _attn_c(qkv_x, qkv_c, nb=nb, seq=seq, ctx_len=ctx_len, tq=512)
    sink_t = jnp.broadcast_to(sink.reshape(KV_D, H_D // KV_D, 1), (KV_D, H_D // KV_D, BLK))
    od = _attn_d(qkv_x, qkv_c, sink_t, nb=nb, seq=seq, ctx_len=ctx_len)
    h = _proj_resid(oc, od, w_out.astype(BF16), h_x, mods_x, norm_g[1], gate_idx=2, tm=512)
    return _mlp(h, mods_x, norm_g[2], norm_g[3], w1.astype(BF16), w2.astype(BF16), tm=512, tf=1024)


def kernel(x, c, ctx, c_ctx, mod_w, mod_b, norm_g, mlp_w1, mlp_w2, ab_w_in, a_w_s, a_b_s, a_ln_g, a_ln_b,
           b_conv_w, b_conv_b, b_a_log, b_dt_bias, b_d, b_norm_g, ab_w_out, cd_w_in, c_q_norm_g, c_k_norm_g,
           d_sink, cd_w_out):
    nb, seq, d = x.shape
    ctx_len = ctx.shape[1]
    depth = mod_w.shape[0]
    assert depth == 2, "the odd layer is implemented as the last layer (no context update)"
    h_x = x.reshape(nb * seq, d)
    h_c = ctx.reshape(nb * ctx_len, d)
    cond_rows = -(-(nb + 1) // 8) * 8
    cond = jnp.zeros((cond_rows, d), F32).at[:nb].set(c).at[nb].set(c_ctx)
    for i in range(depth):
        j = i // 2
        mods = _modvec(cond, mod_w, mod_b, i)
        mods_x = mods[:nb].reshape(nb, N_MOD, d)
        mods_c = mods[nb:nb + 1].reshape(1, N_MOD, d)
        if i % 2 == 0:
            h_x, h_c = _ab_layer(h_x, h_c, mods_x, mods_c, norm_g[i], ab_w_in[j], a_w_s[j], a_b_s[j], a_ln_g[j],
                                 a_ln_b[j], b_conv_w[j], b_conv_b[j], b_a_log[j], b_dt_bias[j], b_d[j],
                                 b_norm_g[j], ab_w_out[j], mlp_w1[i], mlp_w2[i], nb=nb, seq=seq, ctx_len=ctx_len)
        else:
            h_x = _cd_layer(h_x, h_c, mods_x, mods_c, norm_g[i], cd_w_in[j], c_q_norm_g[j], c_k_norm_g[j],
                            d_sink[j], cd_w_out[j], mlp_w1[i], mlp_w2[i], nb=nb, seq=seq, ctx_len=ctx_len)
    return h_x.reshape(nb, seq, d)
```

```python
import functools
import math

import jax
import jax.numpy as jnp
from jax import lax
from jax.experimental import pallas as pl
from jax.experimental.pallas import tpu as pltpu

F32 = jnp.float32
BF16 = jnp.bfloat16

D_MODEL = 2048
GRID_W = 64
BLK = 128
EPS = 1e-6
N_MOD = 6

D_A = 2048
H_A = 8
DH_A = D_A // H_A

D_INNER = 2048
HEAD_P = 64
H_B = D_INNER // HEAD_P
G_B = 4
R_B = H_B // G_B
N_STATE = 128
CONV_W = 5
CONV_DIM = D_INNER + 2 * G_B * N_STATE

HEAD_DIM = 128
H_C = 8
KV_C = 2
H_D = 8
KV_D = 2
ROPE_BASE = 10000.0
AXIS_DIM = HEAD_DIM // 2
D_FF = 4 * D_MODEL
CD_IN = (H_C + 2 * KV_C + H_D + 2 * KV_D) * HEAD_DIM
ATTN_SCALE = HEAD_DIM ** -0.5
LOG2_E = math.log2(math.e)

AB_MAIN = 2 * D_A + D_INNER + CONV_DIM
DT_PAD = 128
SSD_CHUNKS_PER_STEP = 4

VMEM_LIMIT_V7X = 56 * 1024 * 1024


def _params(*sem, vmem=VMEM_LIMIT_V7X):
    return pltpu.CompilerParams(dimension_semantics=sem, vmem_limit_bytes=vmem)


def _rms(x, g):
    return x * lax.rsqrt(jnp.mean(x * x, axis=-1, keepdims=True) + EPS) * g


def _silu(x):
    return x * jax.nn.sigmoid(x)


def _gelu(x):
    return 0.5 * x * (1.0 + lax.erf(x * (1.0 / math.sqrt(2.0))))


def _softplus(x):
    return jnp.maximum(x, 0.0) + jnp.log1p(jnp.exp(-jnp.abs(x)))


def _dot(a, b):
    return jnp.dot(a, b, preferred_element_type=F32)


def _dot_nt(a, b):
    return lax.dot_general(a, b, (((1,), (1,)), ((), ())), preferred_element_type=F32)


def _dot_tn(a, b):
    return lax.dot_general(a, b, (((0,), (0,)), ((), ())), preferred_element_type=F32)


def _chunk_cumsum(x, reverse):
    n = x.shape[0]
    pos = lax.broadcasted_iota(jnp.int32, x.shape, 0) & (BLK - 1)
    k = 1
    while k < BLK:
        if reverse:
            x = x + jnp.where(pos < BLK - k, pltpu.roll(x, n - k, 0), 0.0)
        else:
            x = x + jnp.where(pos >= k, pltpu.roll(x, k, 0), 0.0)
        k *= 2
    return x


def _norm_modulate_rows(x_ref, g_ref, mod_ref, a_ref, shift_idx, scale_idx):
    gs = g_ref[...] * (1.0 + mod_ref[0, scale_idx:scale_idx + 1, :])
    sh = mod_ref[0, shift_idx:shift_idx + 1, :]

    def strip(i, carry):
        r = pl.ds(pl.multiple_of(i * 16, 16), 16)
        x = x_ref[r, :]
        rs = lax.rsqrt(jnp.mean(x * x, axis=-1, keepdims=True) + EPS)
        a_ref[r, :] = (x * rs * gs + sh).astype(a_ref.dtype)
        return carry

    lax.fori_loop(0, x_ref.shape[0] // 16, strip, 0, unroll=8)


def _gated_norm_residual_rows(y_ref, h_ref, g_ref, mod_ref, gate_idx, o_ref):
    gg = mod_ref[0, gate_idx:gate_idx + 1, :] * g_ref[...]

    def strip(i, carry):
        r = pl.ds(pl.multiple_of(i * 8, 8), 8)
        y = y_ref[r, :]
        rs = lax.rsqrt(jnp.mean(y * y, axis=-1, keepdims=True) + EPS)
        o_ref[r, :] = h_ref[r, :] + y * rs * gg
        return carry

    lax.fori_loop(0, y_ref.shape[0] // 8, strip, 0, unroll=16)


def _modvec_body(s_ref, w_ref, b_ref, o_ref):
    a = _silu(s_ref[...]).astype(BF16)
    o_ref[...] = _dot(a, w_ref[...].astype(BF16)) + b_ref[...]


def _modvec(s, w, b, layer):
    r, d = s.shape
    depth, _, n = w.shape
    tn = 1024
    return pl.pallas_call(
        _modvec_body,
        grid=(n // tn,),
        in_specs=[pl.BlockSpec((r, d), lambda j: (0, 0)),
                  pl.BlockSpec((None, d, tn), lambda j: (layer, 0, j)),
                  pl.BlockSpec((None, 1, tn), lambda j: (layer, 0, j))],
        out_specs=pl.BlockSpec((r, tn), lambda j: (0, j)),
        out_shape=jax.ShapeDtypeStruct((r, n), F32),
        compiler_params=_params("arbitrary"),
        name="modvec",
    )(s, w, b.reshape(depth, 1, n))


def _nmm_body(*refs, shift_idx, scale_idx, side, act_tiles):
    if side:
        x_ref, g_ref, mod_ref, w_ref, ws_ref, bs_ref, alog_ref, o_ref, dt_ref, cum_ref, a_ref = refs
    else:
        x_ref, g_ref, mod_ref, w_ref, o_ref, a_ref = refs

    @pl.when(pl.program_id(1) == 0)
    def _():
        _norm_modulate_rows(x_ref, g_ref, mod_ref, a_ref, shift_idx, scale_idx)
        if side:
            dt = _softplus(_dot(a_ref[...], ws_ref[...]) + bs_ref[...])
            dt_ref[...] = dt
            dta = dt * (-jnp.exp(alog_ref[...]))
            lane = lax.broadcasted_iota(jnp.int32, dta.shape, 1)
            cum_ref[...] = jnp.where(lane < H_B, _chunk_cumsum(dta, False), _chunk_cumsum(dta, True))

    if act_tiles is None:
        o_ref[...] = _dot(a_ref[...], w_ref[...]).astype(o_ref.dtype)
    else:
        j = pl.program_id(1)
        n_gelu, n_silu = act_tiles

        @pl.when(j < n_gelu)
        def _():
            o_ref[...] = _gelu(_dot(a_ref[...], w_ref[...])).astype(o_ref.dtype)

        @pl.when((j >= n_gelu) & (j < n_gelu + n_silu))
        def _():
            o_ref[...] = _silu(_dot(a_ref[...], w_ref[...])).astype(o_ref.dtype)

        @pl.when(j >= n_gelu + n_silu)
        def _():
            o_ref[...] = _dot(a_ref[...], w_ref[...]).astype(o_ref.dtype)


def _nmm(x, g, mods, w, w_side=None, b_side=None, alog_side=None, *, shift_idx, scale_idx, out_dtype, tm, tn,
         act_tiles=None, n_cols=None):
    m, k = x.shape
    n = w.shape[1] if n_cols is None else n_cols
    tm = min(tm, m // mods.shape[0])
    tn = min(tn, n)
    rows_per_mod = m // mods.shape[0]
    assert rows_per_mod % tm == 0 and n % tn == 0 and m % tm == 0
    side = w_side is not None
    in_specs = [pl.BlockSpec((tm, k), lambda i, j: (i, 0)),
                pl.BlockSpec((1, k), lambda i, j: (0, 0)),
                pl.BlockSpec((1, N_MOD, k), lambda i, j: ((i * tm) // rows_per_mod, 0, 0)),
                pl.BlockSpec((k, tn), lambda i, j: (0, j))]
    out_specs = [pl.BlockSpec((tm, tn), lambda i, j: (i, j))]
    out_shape = [jax.ShapeDtypeStruct((m, n), out_dtype)]
    args = [x, g.reshape(1, k), mods, w]
    if side:
        ns = w_side.shape[1]
        in_specs += [pl.BlockSpec((k, ns), lambda i, j: (0, 0)), pl.BlockSpec((1, ns), lambda i, j: (0, 0)),
                     pl.BlockSpec((1, ns), lambda i, j: (0, 0))]
        out_specs += [pl.BlockSpec((tm, ns), lambda i, j: (i, 0))] * 2
        out_shape += [jax.ShapeDtypeStruct((m, ns), F32)] * 2
        args += [w_side, b_side, alog_side]
        assert tm % BLK == 0
    out = pl.pallas_call(
        functools.partial(_nmm_body, shift_idx=shift_idx, scale_idx=scale_idx, side=side, act_tiles=act_tiles),
        grid=(m // tm, n // tn),
        in_specs=in_specs,
        out_specs=out_specs,
        out_shape=out_shape,
        scratch_shapes=[pltpu.VMEM((tm, k), BF16)],
        compiler_params=_params("parallel", "arbitrary"),
        name="norm_mod_matmul",
    )(*args)
    return out if side else out[0]


def _gmlp_body(u_ref, v_ref, lng_ref, lnb_ref, ws_ref, bias_ref, o_ref, *, nchunk):
    v = v_ref[...].astype(F32)
    mu = jnp.mean(v, axis=-1, keepdims=True)
    vc = v - mu
    vn = vc * lax.rsqrt(jnp.mean(vc * vc, axis=-1, keepdims=True) + EPS) * lng_ref[...] + lnb_ref[...]
    vb = vn.astype(BF16)
    for c in range(nchunk):
        rows = slice(c * BLK, (c + 1) * BLK)
        for h in range(H_A):
            cols = slice(h * DH_A, (h + 1) * DH_A)
            mix = _dot(ws_ref[h], vb[rows, cols]) + bias_ref[:, cols]
            u = u_ref[rows, cols].astype(F32)
            o_ref[rows, cols] = (u * mix).astype(o_ref.dtype)


def _gmlp(proj, ws, bias, ln_g, ln_b, *, tr):
    m = proj.shape[0]
    tr = min(tr, m)
    return pl.pallas_call(
        functools.partial(_gmlp_body, nchunk=tr // BLK),
        grid=(m // tr,),
        in_specs=[pl.BlockSpec((tr, D_A), lambda i: (i, 0)),
                  pl.BlockSpec((tr, D_A), lambda i: (i, 1)),
                  pl.BlockSpec((1, D_A), lambda i: (0, 0)),
                  pl.BlockSpec((1, D_A), lambda i: (0, 0)),
                  pl.BlockSpec((H_A, BLK, BLK), lambda i: (0, 0, 0)),
                  pl.BlockSpec((BLK, D_A), lambda i: (0, 0))],
        out_specs=pl.BlockSpec((tr, D_A), lambda i: (i, 0)),
        out_shape=jax.ShapeDtypeStruct((m, D_A), BF16),
        compiler_params=_params("parallel"),
        name="gmlp_gate",
    )(proj, proj, ln_g.reshape(1, D_A), ln_b.reshape(1, D_A), ws, bias)


def _conv_body(x_ref, w_ref, b_ref, o_ref, *, seq):
    x = x_ref[...].astype(F32)
    row = lax.broadcasted_iota(jnp.int32, x.shape, 0)
    acc = x * w_ref[CONV_W // 2:CONV_W // 2 + 1, :] + b_ref[...]
    for k in range(CONV_W):
        d = k - CONV_W // 2
        if d == 0:
            continue
        shifted = pltpu.roll(x, (-d) % seq, 0)
        valid = (row >= -d) if d < 0 else (row < seq - d)
        acc = acc + jnp.where(valid, shifted, 0.0) * w_ref[k:k + 1, :]
    o_ref[...] = _silu(acc).astype(o_ref.dtype)


def _conv_silu(proj, w, b, *, nb, seq):
    tc = 512
    col0 = (AB_MAIN - CONV_DIM) // tc
    return pl.pallas_call(
        functools.partial(_conv_body, seq=seq),
        grid=(nb, CONV_DIM // tc),
        in_specs=[pl.BlockSpec((seq, tc), lambda b_, j: (b_, col0 + j)),
                  pl.BlockSpec((CONV_W, tc), lambda b_, j: (0, j)),
                  pl.BlockSpec((1, tc), lambda b_, j: (0, j))],
        out_specs=pl.BlockSpec((seq, tc), lambda b_, j: (b_, j)),
        out_shape=jax.ShapeDtypeStruct((nb * seq, CONV_DIM), BF16),
        compiler_params=_params("parallel", "parallel"),
        name="conv_silu",
    )(proj, w, b.reshape(1, CONV_DIM))


def _split3(v):
    hi = v.astype(BF16)
    r1 = v - hi.astype(F32)
    mid = r1.astype(BF16)
    lo = (r1 - mid.astype(F32)).astype(BF16)
    return jnp.concatenate([hi, mid, lo], axis=1)


def _ssd_body(*refs, direction, nsteps, sub, epilogue):
    if epilogue:
        (xs_ref, b_ref, c_ref, dt_ref, cum_ref, dtt_ref, cumt_ref, xexp_ref, h0_ref,
         yf_ref, z_ref, dskip_ref, gn_ref, y_ref, hout_ref, h_scr, y_scr) = refs
    else:
        (xs_ref, b_ref, c_ref, dt_ref, cum_ref, dtt_ref, cumt_ref, xexp_ref, h0_ref,
         y_ref, hout_ref, h_scr) = refs
    step = pl.program_id(1)

    @pl.when(step == 0)
    def _():
        h_scr[...] = h0_ref[0]

    off = direction * H_B
    ii = lax.broadcasted_iota(jnp.int32, (BLK, BLK), 0)
    jj = lax.broadcasted_iota(jnp.int32, (BLK, BLK), 1)
    tri = (jj <= ii) if direction == 0 else (jj >= ii)
    last = BLK - 1 if direction == 0 else 0
    lane_c = lax.broadcasted_iota(jnp.int32, (BLK, DT_PAD), 1)
    mine = (lane_c >= off) & (lane_c < off + H_B)
    lane_8 = lax.broadcasted_iota(jnp.int32, (8, DT_PAD), 1)
    mine_8 = (lane_8 >= off) & (lane_8 < off + H_B)
    first_head = jj < HEAD_P

    def chunk(k):
        rows = slice(k * BLK, (k + 1) * BLK)
        dt_c = dt_ref[rows, :]
        s_c = cum_ref[rows, :]
        dt_r = dtt_ref[off:off + H_B, rows]
        s_r = cumt_ref[off:off + H_B, rows]
        tot_c = s_c[last:last + 1, :]

        w_c = jnp.where(mine, jnp.exp(tot_c - s_c) * dt_c, 0.0)
        w_exp = _dot(_split3(w_c), xexp_ref[...])
        dec_c = jnp.where(mine_8, jnp.broadcast_to(jnp.exp(tot_c), (8, DT_PAD)), 0.0)
        dec_exp = _dot(_split3(dec_c), xexp_ref[...])[0:1, :]

        xs = xs_ref[rows, :]
        h_in = h_scr[...]
        h_in_b = h_in.astype(BF16)
        y_dst = y_scr if epilogue else y_ref.at[rows, :]

        for g in range(G_B):
            cg = c_ref[rows, g * N_STATE:(g + 1) * N_STATE]
            bg = b_ref[rows, g * N_STATE:(g + 1) * N_STATE]
            cb = _dot_nt(cg, bg)
            cg_f = cg.astype(F32)
            for pair in range(R_B // 2):
                cols = slice((g * (R_B // 2) + pair) * 2 * HEAD_P, (g * (R_B // 2) + pair + 1) * 2 * HEAD_P)
                rhs = jnp.concatenate([xs[:, cols], h_in_b[:, cols]], axis=0)
                outs = []
                for q in range(2):
                    r = g * R_B + pair * 2 + q
                    s_col = jnp.broadcast_to(s_c[:, off + r:off + r + 1], (BLK, BLK))
                    seg = s_col - s_r[r:r + 1, :]
                    decay = jnp.exp(jnp.where(tri, seg, -jnp.inf))
                    m_in = (cb * decay * dt_r[r:r + 1, :]).astype(BF16)
                    m_st = (cg_f * jnp.exp(s_col)).astype(BF16)
                    outs.append(_dot(jnp.concatenate([m_in, m_st], axis=1), rhs))
                y_dst[:, cols] = jnp.where(first_head, outs[0], outs[1])

        xw = (xs.astype(F32) * w_exp).astype(BF16)
        for g in range(G_B):
            cols = slice(g * R_B * HEAD_P, (g + 1) * R_B * HEAD_P)
            bg = b_ref[rows, g * N_STATE:(g + 1) * N_STATE]
            h_scr[:, cols] = h_in[:, cols] * dec_exp[:, cols] + _dot_tn(bg, xw[:, cols])

        if epilogue:
            y = y_scr[...] + yf_ref[rows, :] + dskip_ref[...] * xs.astype(F32)
            y = y * z_ref[rows, :].astype(F32)
            for g in range(G_B):
                cols = slice(g * R_B * HEAD_P, (g + 1) * R_B * HEAD_P)
                y_ref[rows, cols] = _rms(y[:, cols], gn_ref[:, cols]).astype(y_ref.dtype)

    for k in (range(sub) if direction == 0 else reversed(range(sub))):
        chunk(k)

    @pl.when(step == nsteps - 1)
    def _():
        hout_ref[0] = h_scr[...]


def _ssd(xbc, steps, xexp, h0, *, nb, seq, direction, epi=None):
    sub = min(SSD_CHUNKS_PER_STEP, seq // BLK)
    tb = sub * BLK
    assert seq % tb == 0
    ns = seq // tb
    m = nb * seq
    dt, cum, dtt, cumt = steps

    def tok(b_, c_):
        return b_ * ns + (c_ if direction == 0 else ns - 1 - c_)

    const2 = lambda b_, c_: (0, 0)
    in_specs = [pl.BlockSpec((tb, D_INNER), lambda b_, c_: (tok(b_, c_), 0)),
                pl.BlockSpec((tb, G_B * N_STATE), lambda b_, c_: (tok(b_, c_), D_INNER // (G_B * N_STATE))),
                pl.BlockSpec((tb, G_B * N_STATE), lambda b_, c_: (tok(b_, c_), D_INNER // (G_B * N_STATE) + 1)),
                pl.BlockSpec((tb, DT_PAD), lambda b_, c_: (tok(b_, c_), 0)),
                pl.BlockSpec((tb, DT_PAD), lambda b_, c_: (tok(b_, c_), 0)),
                pl.BlockSpec((2 * H_B, tb), lambda b_, c_: (0, tok(b_, c_))),
                pl.BlockSpec((2 * H_B, tb), lambda b_, c_: (0, tok(b_, c_))),
                pl.BlockSpec((3 * DT_PAD, D_INNER), const2),
                pl.BlockSpec((1, N_STATE, D_INNER), lambda b_, c_: (b_, 0, 0))]
    args = [xbc, xbc, xbc, dt, cum, dtt, cumt, xexp[direction], h0]
    scratch = [pltpu.VMEM((N_STATE, D_INNER), F32)]
    if epi is not None:
        y_fwd, proj, d_skip, gn_g = epi
        in_specs += [pl.BlockSpec((tb, D_INNER), lambda b_, c_: (tok(b_, c_), 0)),
                     pl.BlockSpec((tb, D_INNER), lambda b_, c_: (tok(b_, c_), 2 * D_A // D_INNER)),
                     pl.BlockSpec((1, D_INNER), const2),
                     pl.BlockSpec((1, D_INNER), const2)]
        args += [y_fwd, proj, d_skip, gn_g]
        scratch.append(pltpu.VMEM((BLK, D_INNER), F32))
    return pl.pallas_call(
        functools.partial(_ssd_body, direction=direction, nsteps=ns, sub=sub, epilogue=epi is not None),
        grid=(nb, ns),
        in_specs=in_specs,
        out_specs=[pl.BlockSpec((tb, D_INNER), lambda b_, c_: (tok(b_, c_), 0)),
                   pl.BlockSpec((1, N_STATE, D_INNER), lambda b_, c_: (b_, 0, 0))],
        out_shape=[jax.ShapeDtypeStruct((m, D_INNER), BF16 if epi is not None else F32),
                   jax.ShapeDtypeStruct((nb, N_STATE, D_INNER), F32)],
        scratch_shapes=scratch,
        compiler_params=_params("parallel", "arbitrary"),
        name="ssd_bwd_gate" if epi is not None else "ssd_fwd",
    )(*args)


def _proj_resid_body(a1_ref, a2_ref, w_ref, h_ref, mod_ref, g_ref, o_ref, y_ref, *, k1, gate_idx):
    y_ref[...] = _dot(a1_ref[...], w_ref[:k1, :]) + _dot(a2_ref[...], w_ref[k1:, :])
    _gated_norm_residual_rows(y_ref, h_ref, g_ref, mod_ref, gate_idx, o_ref)


def _proj_resid(a1, a2, w, h, mods, g, *, gate_idx, tm):
    m, k1 = a1.shape
    n = w.shape[1]
    tm = min(tm, m // mods.shape[0])
    rows_per_mod = m // mods.shape[0]
    assert rows_per_mod % tm == 0 and a2.shape == a1.shape and w.shape[0] == 2 * k1
    return pl.pallas_call(
        functools.partial(_proj_resid_body, k1=k1, gate_idx=gate_idx),
        grid=(m // tm,),
        in_specs=[pl.BlockSpec((tm, k1), lambda i: (i, 0)),
                  pl.BlockSpec((tm, k1), lambda i: (i, 0)),
                  pl.BlockSpec((2 * k1, n), lambda i: (0, 0), pipeline_mode=pl.Buffered(1)),
                  pl.BlockSpec((tm, n), lambda i: (i, 0)),
                  pl.BlockSpec((1, N_MOD, n), lambda i: ((i * tm) // rows_per_mod, 0, 0)),
                  pl.BlockSpec((1, n), lambda i: (0, 0))],
        out_specs=pl.BlockSpec((tm, n), lambda i: (i, 0)),
        out_shape=jax.ShapeDtypeStruct((m, n), F32),
        scratch_shapes=[pltpu.VMEM((tm, n), F32)],
        compiler_params=_params("parallel"),
        name="proj_resid",
    )(a1, a2, w, h, mods, g.reshape(1, n))


def _mlp_body(h_ref, mod_ref, gpre_ref, gpost_ref, w1_ref, w2_ref, o_ref, a_ref, y_ref, *, nf):
    f = pl.program_id(1)

    @pl.when(f == 0)
    def _():
        _norm_modulate_rows(h_ref, gpre_ref, mod_ref, a_ref, 3, 4)

    half = a_ref.shape[0] // 2

    def partial_out(s):
        hid = jnp.square(jnp.maximum(_dot(a_ref[s * half:(s + 1) * half, :], w1_ref[...]), 0.0)).astype(BF16)
        return _dot(hid, w2_ref[...])

    @pl.when(f == 0)
    def _():
        for s in range(2):
            o_ref[s * half:(s + 1) * half, :] = partial_out(s)

    @pl.when((f > 0) & (f < nf - 1))
    def _():
        for s in range(2):
            o_ref[s * half:(s + 1) * half, :] += partial_out(s)

    @pl.when(f == nf - 1)
    def _():
        for s in range(2):
            rows = slice(s * half, (s + 1) * half)
            y_ref[...] = o_ref[rows, :] + partial_out(s)
            _gated_norm_residual_rows(y_ref, h_ref.at[rows, :], gpost_ref, mod_ref, 5, o_ref.at[rows, :])


def _mlp(h, mods, g_pre, g_post, w1, w2, *, tm, tf):
    m, d = h.shape
    ff = w1.shape[1]
    tm = min(tm, m // mods.shape[0])
    rows_per_mod = m // mods.shape[0]
    assert rows_per_mod % tm == 0 and ff % tf == 0
    nf = ff // tf
    assert nf >= 2 and tm % 16 == 0
    return pl.pallas_call(
        functools.partial(_mlp_body, nf=nf),
        grid=(m // tm, nf),
        in_specs=[pl.BlockSpec((tm, d), lambda i, f: (i, 0)),
                  pl.BlockSpec((1, N_MOD, d), lambda i, f: ((i * tm) // rows_per_mod, 0, 0)),
                  pl.BlockSpec((1, d), lambda i, f: (0, 0)),
                  pl.BlockSpec((1, d), lambda i, f: (0, 0)),
                  pl.BlockSpec((d, tf), lambda i, f: (0, f)),
                  pl.BlockSpec((tf, d), lambda i, f: (f, 0))],
        out_specs=pl.BlockSpec((tm, d), lambda i, f: (i, 0)),
        out_shape=jax.ShapeDtypeStruct((m, d), F32),
        scratch_shapes=[pltpu.VMEM((tm, d), BF16), pltpu.VMEM((tm // 2, d), F32)],
        compiler_params=_params("parallel", "arbitrary"),
        name="mlp_sublayer",
    )(h, mods, g_pre.reshape(1, d), g_post.reshape(1, d), w1, w2)


_QC_HEADS = range(0, H_C)
_KC_HEADS = range(H_C, H_C + KV_C)
_QD_HEADS = range(H_C + 2 * KV_C, H_C + 2 * KV_C + H_D)
_KD_HEADS = range(H_C + 2 * KV_C + H_D, H_C + 2 * KV_C + H_D + KV_D)


def _qkv_prep_body(x_ref, cos_ref, sin_ref, qg_ref, kg_ref, perm_ref, o_ref, *, rope):
    lane = lax.broadcasted_iota(jnp.int32, (x_ref.shape[0], HEAD_DIM), 1)
    first_half = (lane % AXIS_DIM) < (AXIS_DIM // 2)
    for hd in range(CD_IN // HEAD_DIM):
        cols = slice(hd * HEAD_DIM, (hd + 1) * HEAD_DIM)
        x = x_ref[:, cols].astype(F32)
        if hd in _QC_HEADS:
            x = _rms(x, qg_ref[...])
        elif hd in _KC_HEADS:
            x = _rms(x, kg_ref[...])
        if rope and (hd in _QC_HEADS or hd in _KC_HEADS):
            partner = jnp.where(first_half,
                                pltpu.roll(x, HEAD_DIM - AXIS_DIM // 2, 1),
                                pltpu.roll(x, AXIS_DIM // 2, 1))
            x = x * cos_ref[...] + partner * sin_ref[...]
        elif rope and (hd in _QD_HEADS or hd in _KD_HEADS):
            partner = _dot(x_ref[:, cols], perm_ref[...])
            x = x * cos_ref[...] + partner * sin_ref[...]
        o_ref[:, cols] = x.astype(o_ref.dtype)


def _qkv_prep(proj, cos, sin_signed, q_g, k_g, *, seq, rope, tr):
    m = proj.shape[0]
    tr = min(tr, seq)
    per_seq = seq // tr
    lane = jnp.arange(HEAD_DIM)
    partner_lane = jnp.where((lane % AXIS_DIM) < AXIS_DIM // 2, lane + AXIS_DIM // 2, lane - AXIS_DIM // 2)
    perm = (lane[:, None] == partner_lane[None, :]).astype(BF16)
    return pl.pallas_call(
        functools.partial(_qkv_prep_body, rope=rope),
        grid=(m // tr,),
        in_specs=[pl.BlockSpec((tr, CD_IN), lambda i: (i, 0)),
                  pl.BlockSpec((tr, HEAD_DIM), lambda i: (i % per_seq, 0)),
                  pl.BlockSpec((tr, HEAD_DIM), lambda i: (i % per_seq, 0)),
                  pl.BlockSpec((1, HEAD_DIM), lambda i: (0, 0)),
                  pl.BlockSpec((1, HEAD_DIM), lambda i: (0, 0)),
                  pl.BlockSpec((HEAD_DIM, HEAD_DIM), lambda i: (0, 0))],
        out_specs=pl.BlockSpec((tr, CD_IN), lambda i: (i, 0)),
        out_shape=jax.ShapeDtypeStruct((m, CD_IN), BF16),
        compiler_params=_params("parallel"),
        name="qkv_prep",
    )(proj, cos, sin_signed, q_g.reshape(1, HEAD_DIM), k_g.reshape(1, HEAD_DIM), perm)


def _attn_c_body(q_ref, k_ref, v_ref, kc_ref, vc_ref, o_ref):
    k = k_ref[...]
    v = v_ref[...]
    kc = kc_ref[...]
    vc = vc_ref[...]
    for r in range(H_C // KV_C):
        cols = slice(r * HEAD_DIM, (r + 1) * HEAD_DIM)
        q = q_ref[:, cols]
        s1 = _dot_nt(q, k)
        s2 = _dot_nt(q, kc)
        mx = jnp.maximum(jnp.max(s1, axis=-1, keepdims=True), jnp.max(s2, axis=-1, keepdims=True))
        p1 = jnp.exp2((s1 - mx) * (ATTN_SCALE * LOG2_E))
        p2 = jnp.exp2((s2 - mx) * (ATTN_SCALE * LOG2_E))
        den = jnp.sum(p1, axis=-1, keepdims=True) + jnp.sum(p2, axis=-1, keepdims=True)
        o = _dot(p1.astype(BF16), v) + _dot(p2.astype(BF16), vc)
        o_ref[:, cols] = (o / den).astype(o_ref.dtype)


def _attn_c(qkv, qkv_ctx, *, nb, seq, ctx_len, tq):
    tq = min(tq, seq)
    nq = seq // tq
    rq = H_C // KV_C
    k0 = H_C
    v0 = H_C + KV_C
    return pl.pallas_call(
        _attn_c_body,
        grid=(nb, KV_C, nq),
        in_specs=[pl.BlockSpec((tq, rq * HEAD_DIM), lambda b_, h, i: (b_ * nq + i, h)),
                  pl.BlockSpec((seq, HEAD_DIM), lambda b_, h, i: (b_, k0 + h)),
                  pl.BlockSpec((seq, HEAD_DIM), lambda b_, h, i: (b_, v0 + h)),
                  pl.BlockSpec((ctx_len, HEAD_DIM), lambda b_, h, i: (b_, k0 + h)),
                  pl.BlockSpec((ctx_len, HEAD_DIM), lambda b_, h, i: (b_, v0 + h))],
        out_specs=pl.BlockSpec((tq, rq * HEAD_DIM), lambda b_, h, i: (b_ * nq + i, h)),
        out_shape=jax.ShapeDtypeStruct((nb * seq, H_C * HEAD_DIM), BF16),
        compiler_params=_params("parallel", "parallel", "arbitrary"),
        name="attn_full",
    )(qkv, qkv, qkv, qkv_ctx, qkv_ctx)


def _attn_d_body(q_ref, kp_ref, kn_ref, kx_ref, vp_ref, vn_ref, vx_ref, kc_ref, vc_ref, sink_ref, o_ref, *, ntile):
    t = pl.program_id(2)
    rq = H_D // KV_D
    stack = rq
    tq = q_ref.shape[0]
    nloc = tq + 2 * BLK
    k_all = jnp.concatenate([kp_ref[...], kn_ref[...], kx_ref[...], kc_ref[...]], axis=0)
    v_all = jnp.concatenate([vp_ref[...], vn_ref[...], vx_ref[...], vc_ref[...]], axis=0)

    ii = lax.broadcasted_iota(jnp.int32, (tq, nloc), 0)
    jj = lax.broadcasted_iota(jnp.int32, (tq, nloc), 1)
    d = jj - ii
    lo = jnp.where(t == 0, BLK, 0)
    hi = jnp.where(t == ntile - 1, nloc - BLK, nloc)
    neg = -jnp.inf
    bias = jnp.where(d >= 0, jnp.where(d <= 2 * BLK, jnp.where(jj >= lo, jnp.where(jj < hi, 0.0, neg), neg), neg),
                     neg)
    bias = jnp.concatenate([bias] * stack, axis=0)
    for r0 in range(0, rq, stack):
        heads = range(r0, r0 + stack)
        q = jnp.concatenate([q_ref[:, r * HEAD_DIM:(r + 1) * HEAD_DIM] for r in heads], axis=0)
        s = _dot_nt(q, k_all) * ATTN_SCALE
        s_loc = s[:, :nloc] + bias
        s_ctx = s[:, nloc:]
        sk = jnp.concatenate([jnp.broadcast_to(sink_ref[0, r:r + 1, 0:1], (tq, 1)) for r in heads], axis=0)
        mx = jnp.maximum(jnp.maximum(jnp.max(s_loc, axis=-1, keepdims=True), jnp.max(s_ctx, axis=-1, keepdims=True)),
                         sk)
        p_loc = jnp.exp(s_loc - mx)
        p_ctx = jnp.exp(s_ctx - mx)
        den = jnp.sum(p_loc, axis=-1, keepdims=True) + jnp.sum(p_ctx, axis=-1, keepdims=True) + jnp.exp(sk - mx)
        p = jnp.concatenate([p_loc.astype(BF16), p_ctx.astype(BF16)], axis=1)
        o = _dot(p, v_all) / den
        for n, r in enumerate(heads):
            o_ref[:, r * HEAD_DIM:(r + 1) * HEAD_DIM] = o[n * tq:(n + 1) * tq].astype(o_ref.dtype)


def _attn_d(qkv, qkv_ctx, sink, *, nb, seq, ctx_len):
    nblk = seq // BLK
    tq = 2 * BLK
    ntile = seq // tq
    rq = H_D // KV_D
    q0 = (H_C + 2 * KV_C) // rq
    k0 = H_C + 2 * KV_C + H_D
    v0 = k0 + KV_D

    def edge(col0, shift):
        def index(b_, h, t):
            return (b_ * nblk + jnp.clip(2 * t + shift, 0, nblk - 1), col0 + h)
        return pl.BlockSpec((BLK, HEAD_DIM), index)

    def tile(col0):
        return pl.BlockSpec((tq, HEAD_DIM), lambda b_, h, t: (b_ * ntile + t, col0 + h))

    return pl.pallas_call(
        functools.partial(_attn_d_body, ntile=ntile),
        grid=(nb, KV_D, ntile),
        in_specs=[pl.BlockSpec((tq, rq * HEAD_DIM), lambda b_, h, t: (b_ * ntile + t, q0 + h)),
                  edge(k0, -1), tile(k0), edge(k0, 2),
                  edge(v0, -1), tile(v0), edge(v0, 2),
                  pl.BlockSpec((ctx_len, HEAD_DIM), lambda b_, h, t: (b_, k0 + h)),
                  pl.BlockSpec((ctx_len, HEAD_DIM), lambda b_, h, t: (b_, v0 + h)),
                  pl.BlockSpec((1, rq, BLK), lambda b_, h, t: (h, 0, 0))],
        out_specs=pl.BlockSpec((tq, rq * HEAD_DIM), lambda b_, h, t: (b_ * ntile + t, h)),
        out_shape=jax.ShapeDtypeStruct((nb * seq, H_D * HEAD_DIM), BF16),
        compiler_params=_params("parallel", "parallel", "arbitrary"),
        name="attn_window",
    )(qkv, qkv, qkv, qkv, qkv, qkv, qkv, qkv_ctx, qkv_ctx, sink)


def _rope_tables(seq):
    t = jnp.arange(seq)
    pos = jnp.stack([t // GRID_W, t % GRID_W], axis=-1).astype(F32)
    inv_freq = ROPE_BASE ** (-jnp.arange(0, AXIS_DIM, 2, dtype=F32) / AXIS_DIM)
    ang = pos[:, :, None] * inv_freq
    cos, sin = jnp.cos(ang), jnp.sin(ang)
    cos_t = jnp.concatenate([cos[:, 0], cos[:, 0], cos[:, 1], cos[:, 1]], axis=-1)
    sin_t = jnp.concatenate([-sin[:, 0], sin[:, 0], -sin[:, 1], sin[:, 1]], axis=-1)
    return cos_t, sin_t


def _pad_dt(v):
    return jnp.zeros((1, DT_PAD), F32).at[0, :2 * H_B].set(v.reshape(-1))


def _head_expansion():
    out = []
    head_of_lane = jnp.arange(D_INNER) // HEAD_P
    for direction in range(2):
        rows = jnp.arange(DT_PAD)[:, None] - direction * H_B
        sel = (rows == head_of_lane[None, :]).astype(BF16)
        out.append(jnp.concatenate([sel, sel, sel], axis=0))
    return out


def _ab_layer(h_x, h_c, mods_x, mods_c, norm_g, w_in, w_s, b_s, ln_g, ln_b, conv_w, conv_b, a_log, dt_bias,
              d_skip, gn_g, w_out, w1, w2, *, nb, seq, ctx_len):
    w_main = w_in.astype(BF16)
    w_dt = jnp.zeros((D_MODEL, DT_PAD), F32).at[:, :2 * H_B].set(w_in[:, AB_MAIN:]).astype(BF16)
    ws = w_s.astype(BF16)
    bias = jnp.repeat(b_s.T, DH_A, axis=1)
    xexp = _head_expansion()
    b_dt = _pad_dt(dt_bias)
    alog_dt = _pad_dt(a_log)
    dskip = jnp.repeat(d_skip, HEAD_P).reshape(1, D_INNER)
    gn = gn_g.reshape(1, D_INNER)
    w_out_b = w_out.astype(BF16)
    w1_b = w1.astype(BF16)
    w2_b = w2.astype(BF16)

    def project(h, mods):
        proj, dt, cum = _nmm(h, norm_g[0], mods, w_main, w_dt, b_dt, alog_dt, shift_idx=0, scale_idx=1,
                             out_dtype=BF16, tm=1024, tn=1024, act_tiles=(2 * D_A // 1024, D_INNER // 1024),
                             n_cols=AB_MAIN)
        return proj, (dt, cum, dt[:, :2 * H_B].T, cum[:, :2 * H_B].T)

    def scans(proj, steps, n_tok, h0):
        xbc = _conv_silu(proj, conv_w, conv_b, nb=nb, seq=n_tok)
        y_f, hf = _ssd(xbc, steps, xexp, h0[0], nb=nb, seq=n_tok, direction=0)
        y_b, hb = _ssd(xbc, steps, xexp, h0[1], nb=nb, seq=n_tok, direction=1, epi=(y_f, proj, dskip, gn))
        return y_b, (hf, hb)

    def finish(h, mods, ya, yb):
        h = _proj_resid(ya, yb, w_out_b, h, mods, norm_g[1], gate_idx=2, tm=512)
        return _mlp(h, mods, norm_g[2], norm_g[3], w1_b, w2_b, tm=1024, tf=512)

    proj_c, steps_c = project(h_c, mods_c)
    proj_x, steps_x = project(h_x, mods_x)
    zero = jnp.zeros((nb, N_STATE, D_INNER), F32)
    yb_c, st_c = scans(proj_c, steps_c, ctx_len, (zero, zero))
    yb_x, _ = scans(proj_x, steps_x, seq, st_c)
    ya_c = _gmlp(proj_c, ws, bias, ln_g, ln_b, tr=512)
    ya_x = _gmlp(proj_x, ws, bias, ln_g, ln_b, tr=512)
    return finish(h_x, mods_x, ya_x, yb_x), finish(h_c, mods_c, ya_c, yb_c)


def _cd_layer(h_x, h_c, mods_x, mods_c, norm_g, w_in, q_g, k_g, sink, w_out, w1, w2, *, nb, seq, ctx_len):
    w_in_b = w_in.astype(BF16)
    cos_t, sin_t = _rope_tables(seq)
    proj_x = _nmm(h_x, norm_g[0], mods_x, w_in_b, shift_idx=0, scale_idx=1, out_dtype=BF16, tm=1024, tn=1024)
    proj_c = _nmm(h_c, norm_g[0], mods_c, w_in_b, shift_idx=0, scale_idx=1, out_dtype=BF16, tm=1024, tn=1024)
    qkv_x = _qkv_prep(proj_x, cos_t, sin_t, q_g, k_g, seq=seq, rope=True, tr=512)
    qkv_c = _qkv_prep(proj_c, cos_t, sin_t, q_g, k_g, seq=ctx_len, rope=False, tr=256)
    oc = _attn_c(qkv_x, qkv_c, nb=nb, seq=seq, ctx_len=ctx_len, tq=1024)
    sink_t = jnp.broadcast_to(sink.reshape(KV_D, H_D // KV_D, 1), (KV_D, H_D // KV_D, BLK))
    od = _attn_d(qkv_x, qkv_c, sink_t, nb=nb, seq=seq, ctx_len=ctx_len)
    h = _proj_resid(oc, od, w_out.astype(BF16), h_x, mods_x, norm_g[1], gate_idx=2, tm=512)
    return _mlp(h, mods_x, norm_g[2], norm_g[3], w1.astype(BF16), w2.astype(BF16), tm=1024, tf=512)


def kernel(x, c, ctx, c_ctx, mod_w, mod_b, norm_g, mlp_w1, mlp_w2, ab_w_in, a_w_s, a_b_s, a_ln_g, a_ln_b,
           b_conv_w, b_conv_b, b_a_log, b_dt_bias, b_d, b_norm_g, ab_w_out, cd_w_in, c_q_norm_g, c_k_norm_g,
           d_sink, cd_w_out):
    nb, seq, d = x.shape
    ctx_len = ctx.shape[1]
    depth = mod_w.shape[0]
    assert depth == 2, "the odd layer is implemented as the last layer (no context update)"
    h_x = x.reshape(nb * seq, d)
    h_c = ctx.reshape(nb * ctx_len, d)
    cond_rows = -(-(nb + 1) // 8) * 8
    cond = jnp.zeros((cond_rows, d), F32).at[:nb].set(c).at[nb].set(c_ctx)
    for i in range(depth):
        j = i // 2
        mods = _modvec(cond, mod_w, mod_b, i)
        mods_x = mods[:nb].reshape(nb, N_MOD, d)
        mods_c = mods[nb:nb + 1].reshape(1, N_MOD, d)
        if i % 2 == 0:
            h_x, h_c = _ab_layer(h_x, h_c, mods_x, mods_c, norm_g[i], ab_w_in[j], a_w_s[j], a_b_s[j], a_ln_g[j],
                                 a_ln_b[j], b_conv_w[j], b_conv_b[j], b_a_log[j], b_dt_bias[j], b_d[j],
                                 b_norm_g[j], ab_w_out[j], mlp_w1[i], mlp_w2[i], nb=nb, seq=seq, ctx_len=ctx_len)
        else:
            h_x = _cd_layer(h_x, h_c, mods_x, mods_c, norm_g[i], cd_w_in[j], c_q_norm_g[j], c_k_norm_g[j],
                            d_sink[j], cd_w_out[j], mlp_w1[i], mlp_w2[i], nb=nb, seq=seq, ctx_len=ctx_len)
    return h_x.reshape(nb, seq, d)
```
